```python
import math
import jax, jax.numpy as jnp
from jax import lax
import numpy as np

D_MODEL = 1024
BATCH = 8
SEQ = 2048
DEPTH = 1
DEC_BATCH = 128
DEC_SEQ = 8
PAST_LEN = 8192
PAGE_SIZE = 128

D_INNER = 2 * D_MODEL
SSM_HEAD_DIM = 64
SSM_HEADS = D_INNER // SSM_HEAD_DIM
SSM_GROUPS = 4
D_STATE = 128
CONV_W = 4
CONV_DIM = D_INNER + 2 * SSM_GROUPS * D_STATE
SSD_CHUNK = 128
SB_HEAD_DIM = 64
SB_HEADS = D_MODEL // SB_HEAD_DIM
SB_WIDTH = SB_HEADS * SB_HEAD_DIM
SB_BLOCK = 128
SB_BIAS_INIT = -6.0
MEM_TOKENS = 256
MEM_HEADS = 4
MEM_HEAD_DIM = D_MODEL // MEM_HEADS
MEM_WIDTH = MEM_HEADS * MEM_HEAD_DIM
PEER_HEADS = 8
N_KEYS = 128
N_EXPERTS = N_KEYS * N_KEYS
PEER_TOPK = 16
PEER_KEY_DIM = 256
PEER_HALF = PEER_KEY_DIM // 2
PEER_BLOCK = 128
N_BRANCHES = 3
RMS_EPS = 1e-6
IN_SPLITS = (D_INNER,
             D_INNER + CONV_DIM,
             D_INNER + CONV_DIM + SSM_HEADS,
             D_INNER + CONV_DIM + SSM_HEADS + 3 * SB_WIDTH,
             D_INNER + CONV_DIM + SSM_HEADS + 3 * SB_WIDTH + MEM_WIDTH)
IN_COLS = D_INNER + CONV_DIM + SSM_HEADS + 3 * SB_WIDTH + MEM_WIDTH + N_BRANCHES * D_MODEL

kernel_name = 'hybrid_ssd_stickbreak_peer_step'

F32 = jnp.float32


def rmsnorm(x, w):
    xf = x.astype(F32)
    xf = xf * lax.rsqrt(jnp.mean(xf * xf, axis=-1, keepdims=True) + RMS_EPS)
    return xf.astype(x.dtype) * w


def gated_group_rmsnorm(y, z, w):
    g = y.astype(F32) * jax.nn.silu(z.astype(F32))
    gs = g.reshape(g.shape[:-1] + (SSM_GROUPS, D_INNER // SSM_GROUPS))
    gs = gs * lax.rsqrt(jnp.mean(gs * gs, axis=-1, keepdims=True) + RMS_EPS)
    return gs.reshape(g.shape).astype(z.dtype) * w


def causal_dwconv(xbc, prev, w, b):
    full = jnp.concatenate([prev.astype(xbc.dtype), xbc], axis=1)
    out = lax.conv_general_dilated(full, w[:, None, :].astype(xbc.dtype), window_strides=(1,),
                                   padding='VALID', dimension_numbers=('NWC', 'WIO', 'NWC'),
                                   feature_group_count=CONV_DIM)
    return out + b, full[:, full.shape[1] - (CONV_W - 1):]


def ssd_scan(xs, dt, a, bm, cm, init_state):
    b, l, h, p = xs.shape
    g, n = bm.shape[2], bm.shape[3]
    r = h // g
    q = min(SSD_CHUNK, l)
    nc = -(-l // q)
    pad = nc * q - l
    xdt = xs.astype(F32) * dt[..., None]
    da = dt * a
    bm = bm.astype(F32)
    cm = cm.astype(F32)
    if pad:
        zpad = lambda t: jnp.pad(t, [(0, 0), (0, pad)] + [(0, 0)] * (t.ndim - 2))
        xdt, da, bm, cm = zpad(xdt), zpad(da), zpad(bm), zpad(cm)
    xdt = xdt.reshape(b, nc, q, g, r, p)
    da = da.reshape(b, nc, q, g, r)
    bm = bm.reshape(b, nc, q, g, n)
    cm = cm.reshape(b, nc, q, g, n)
    cs = jnp.cumsum(da, axis=2)
    causal = jnp.tril(jnp.ones((q, q), bool))[None, None, :, :, None, None]
    seg = cs[:, :, :, None] - cs[:, :, None, :]
    decay = jnp.exp(jnp.where(causal, seg, -jnp.inf))
    cb = jnp.einsum('bctgn,bcsgn->bctsg', cm, bm)
    y_diag = jnp.einsum('bctsgr,bcsgrp->bctgrp', cb[..., None] * decay, xdt)
    to_end = jnp.exp(cs[:, :, -1:] - cs)
    chunk_states = jnp.einsum('bcsgn,bcsgr,bcsgrp->bcgrpn', bm, to_end, xdt)
    chunk_decay = jnp.exp(cs[:, :, -1])

    def carry_step(s, inp):
        st, dec = inp
        return s * dec[..., None, None] + st, s

    s0 = init_state.astype(F32).reshape(b, g, r, p, n)
    final, prev = lax.scan(carry_step, s0, (jnp.moveaxis(chunk_states, 1, 0), jnp.moveaxis(chunk_decay, 1, 0)))
    prev = jnp.moveaxis(prev, 0, 1)
    y_off = jnp.einsum('bctgn,bcgrpn,bctgr->bctgrp', cm, prev, jnp.exp(cs))
    y = (y_diag + y_off).reshape(b, nc * q, h, p)[:, :l]
    return y, final.reshape(b, h, p, n)


def stick_breaking(q, k, v, bias, pos0):
    b, l, nh, dh = q.shape
    tk = k.shape[1]
    qb = SB_BLOCK if l % SB_BLOCK == 0 else l
    nb = l // qb
    q_blocks = jnp.moveaxis(q.reshape(b, nb, qb, nh, dh), 1, 0)
    q_pos = (pos0 + jnp.arange(l)).reshape(nb, qb)
    k_pos = jnp.arange(tk)
    kf = k.astype(F32)
    vf = v.astype(F32)
    bf = bias.astype(F32)[None, :, None, None]
    scale = dh ** -0.5

    def one_block(args):
        qblk, qp = args
        z = jnp.einsum('bqhd,bkhd->bhqk', qblk.astype(F32), kf) * scale + bf
        mask = k_pos[None, :] < qp[:, None]
        log_keep = jnp.where(mask, jax.nn.log_sigmoid(-z), 0.0)
        later = lax.cumsum(log_keep, axis=3, reverse=True) - log_keep
        w = jnp.where(mask, jnp.exp(jax.nn.log_sigmoid(z) + later), 0.0)
        return jnp.einsum('bhqk,bkhd->bqhd', w, vf)

    o = lax.map(one_block, (q_blocks, q_pos))
    return jnp.moveaxis(o, 0, 1).reshape(b, l, nh, dh).astype(q.dtype)


def memory_kv(mem, lw):
    b, mt, _ = mem.shape
    kv = rmsnorm(mem, lw['mem_norm_w']) @ lw['w_mem_kv']
    k, v = jnp.split(kv, 2, axis=-1)
    k = rmsnorm(k.reshape(b, mt, MEM_HEADS, MEM_HEAD_DIM), lw['k_norm_w'])
    return k, v.reshape(b, mt, MEM_HEADS, MEM_HEAD_DIM)


def memory_attend(q, mem_k, mem_v):
    s = jnp.einsum('blhd,bmhd->bhlm', q.astype(F32), mem_k.astype(F32)) * MEM_HEAD_DIM ** -0.5
    p = jax.nn.softmax(s, axis=-1)
    return jnp.einsum('bhlm,bmhd->blhd', p, mem_v.astype(F32)).astype(q.dtype)


def peer_ffn(h, w_q, sub_keys, w_up, w_down):
    b, l, d = h.shape
    t = h.reshape(b * l, d)
    n_tok = t.shape[0]
    q = (t @ w_q).astype(F32).reshape(n_tok, PEER_HEADS, 2, PEER_HALF)
    s = jnp.einsum('thcd,ckd->thck', q, sub_keys.astype(F32))
    half_sc, half_ix = lax.top_k(s, PEER_TOPK)
    cand_sc = (half_sc[:, :, 0, :, None] + half_sc[:, :, 1, None, :]).reshape(n_tok, PEER_HEADS, PEER_TOPK * PEER_TOPK)
    cand_ix = (half_ix[:, :, 0, :, None] * N_KEYS + half_ix[:, :, 1, None, :]).reshape(n_tok, PEER_HEADS, PEER_TOPK * PEER_TOPK)
    top_sc, pos = lax.top_k(cand_sc, PEER_TOPK)
    expert = jnp.take_along_axis(cand_ix, pos, axis=-1)
    gate = jax.nn.softmax(top_sc, axis=-1)
    nb = -(-n_tok // PEER_BLOCK)
    pad = nb * PEER_BLOCK - n_tok
    t_b = jnp.pad(t, ((0, pad), (0, 0))).reshape(nb, PEER_BLOCK, d)
    e_b = jnp.pad(expert, ((0, pad), (0, 0), (0, 0))).reshape(nb, PEER_BLOCK, PEER_HEADS, PEER_TOPK)
    g_b = jnp.pad(gate, ((0, pad), (0, 0), (0, 0))).reshape(nb, PEER_BLOCK, PEER_HEADS, PEER_TOPK)

    def one_block(args):
        tb, eb, gb = args
        act = jax.nn.gelu(jnp.einsum('td,thkd->thk', tb, w_up[eb]), approximate=False)
        coef = (gb * act.astype(F32)).astype(w_down.dtype)
        return jnp.einsum('thk,thkd->td', coef, w_down[eb])

    out = lax.map(one_block, (t_b, e_b, g_b))
    return out.reshape(nb * PEER_BLOCK, d)[:n_tok].reshape(b, l, d).astype(h.dtype)


def decoder_layer(x, mem_k, mem_v, sb_k_past, sb_v_past, conv_prev, ssm_prev, lw):
    b, l, _ = x.shape
    h = rmsnorm(x, lw['norm1_w'])
    z, xbc, dt_raw, qkv, q_mem, gates = jnp.split(h @ lw['w_in'], IN_SPLITS, axis=-1)
    xbc, conv_new = causal_dwconv(xbc, conv_prev, lw['conv_w'], lw['conv_b'])
    xbc = jax.nn.silu(xbc)
    xs, bm, cm = jnp.split(xbc, [D_INNER, D_INNER + SSM_GROUPS * D_STATE], axis=-1)
    xs = xs.reshape(b, l, SSM_HEADS, SSM_HEAD_DIM)
    dt = jax.nn.softplus(dt_raw.astype(F32) + lw['dt_bias'].astype(F32))
    a = -jnp.exp(lw['a_log'].astype(F32))
    y, ssm_new = ssd_scan(xs, dt, a, bm.reshape(b, l, SSM_GROUPS, D_STATE),
                          cm.reshape(b, l, SSM_GROUPS, D_STATE), ssm_prev)
    y = y + lw['d_skip'].astype(F32)[:, None] * xs.astype(F32)
    y = gated_group_rmsnorm(y.reshape(b, l, D_INNER), z, lw['ssm_norm_w'])
    branch_ssm = y @ lw['w_ssm_out']
    q_sb, k_sb, v_sb = jnp.split(qkv, 3, axis=-1)
    q_sb = q_sb.reshape(b, l, SB_HEADS, SB_HEAD_DIM)
    k_sb = k_sb.reshape(b, l, SB_HEADS, SB_HEAD_DIM)
    v_sb = v_sb.reshape(b, l, SB_HEADS, SB_HEAD_DIM)
    keys = jnp.concatenate([sb_k_past.astype(k_sb.dtype), k_sb], axis=1)
    vals = jnp.concatenate([sb_v_past.astype(v_sb.dtype), v_sb], axis=1)
    o_sb = stick_breaking(q_sb, keys, vals, lw['sb_bias'], sb_k_past.shape[1])
    branch_sb = o_sb.reshape(b, l, SB_WIDTH) @ lw['w_sb_out']
    qm = rmsnorm(q_mem.reshape(b, l, MEM_HEADS, MEM_HEAD_DIM), lw['q_norm_w'])
    o_mem = memory_attend(qm, mem_k, mem_v)
    branch_mem = o_mem.reshape(b, l, MEM_WIDTH) @ lw['w_mem_out']
    g_ssm, g_sb, g_mem = jnp.split(jax.nn.sigmoid(gates), N_BRANCHES, axis=-1)
    merged = g_ssm * branch_ssm + g_sb * branch_sb + g_mem * branch_mem
    x = x + merged @ lw['w_out']
    x = x + peer_ffn(rmsnorm(x, lw['norm2_w']), lw['peer_w_q'], lw['peer_sub_keys'],
                     lw['peer_w_up'], lw['peer_w_down'])
    return x, k_sb, v_sb, conv_new, ssm_new.astype(ssm_prev.dtype)


def setup_inputs(seed: int = 0) -> dict:
    key = jax.random.key(seed)
    ks = jax.random.split(key, 32)
    n_pages = PAST_LEN // PAGE_SIZE
    n_used = DEC_BATCH * n_pages
    n_pool = n_used + (n_used + 3) // 4

    def nrm(i, shape, scale=1.0):
        return jax.random.normal(ks[i], shape, F32) * scale

    def gain(i, shape):
        return 1.0 + nrm(i, shape, 0.01)

    page_table = jax.random.permutation(ks[8], n_pool)[:n_used].reshape(DEC_BATCH, n_pages).astype(jnp.int32)
    dt0 = jnp.exp(jax.random.uniform(ks[14], (DEPTH, SSM_HEADS), F32, math.log(1e-3), math.log(1e-1)))
    dt_bias = dt0 + jnp.log(-jnp.expm1(-dt0))
    a_log = jnp.log(jax.random.uniform(ks[15], (DEPTH, SSM_HEADS), F32, 1.0, 16.0))
    return {
        'x_prompt': nrm(0, (BATCH, SEQ, D_MODEL)),
        'x_sample': nrm(1, (DEC_BATCH, DEC_SEQ, D_MODEL)),
        'cache_sb_k': nrm(2, (DEPTH, n_pool, PAGE_SIZE, SB_HEADS, SB_HEAD_DIM)),
        'cache_sb_v': nrm(3, (DEPTH, n_pool, PAGE_SIZE, SB_HEADS, SB_HEAD_DIM)),
        'cache_mem_k': nrm(4, (DEPTH, DEC_BATCH, MEM_TOKENS, MEM_HEADS, MEM_HEAD_DIM)),
        'cache_mem_v': nrm(5, (DEPTH, DEC_BATCH, MEM_TOKENS, MEM_HEADS, MEM_HEAD_DIM)),
        'state_conv': nrm(6, (DEPTH, DEC_BATCH, CONV_W - 1, CONV_DIM)),
        'state_ssm': nrm(7, (DEPTH, DEC_BATCH, SSM_HEADS, SSM_HEAD_DIM, D_STATE), 0.1),
        'page_table': page_table,
        'mem_prompt': nrm(9, (BATCH, MEM_TOKENS, D_MODEL)),
        'norm1_w': gain(10, (DEPTH, D_MODEL)),
        'w_in': nrm(11, (DEPTH, D_MODEL, IN_COLS), D_MODEL ** -0.5),
        'conv_w': nrm(12, (DEPTH, CONV_W, CONV_DIM), CONV_W ** -0.5),
        'conv_b': nrm(13, (DEPTH, CONV_DIM), 0.01),
        'dt_bias': dt_bias,
        'a_log': a_log,
        'd_skip': gain(16, (DEPTH, SSM_HEADS)),
        'ssm_norm_w': gain(17, (DEPTH, D_INNER)),
        'w_ssm_out': nrm(18, (DEPTH, D_INNER, D_MODEL), D_INNER ** -0.5),
        'w_sb_out': nrm(19, (DEPTH, SB_WIDTH, D_MODEL), SB_WIDTH ** -0.5),
        'sb_bias': SB_BIAS_INIT + nrm(31, (DEPTH, SB_HEADS), 0.1),
        'mem_norm_w': gain(20, (DEPTH, D_MODEL)),
        'w_mem_kv': nrm(21, (DEPTH, D_MODEL, 2 * MEM_WIDTH), D_MODEL ** -0.5),
        'q_norm_w': gain(22, (DEPTH, MEM_HEAD_DIM)),
        'k_norm_w': gain(23, (DEPTH, MEM_HEAD_DIM)),
        'w_mem_out': nrm(24, (DEPTH, MEM_WIDTH, D_MODEL), MEM_WIDTH ** -0.5),
        'w_out': nrm(25, (DEPTH, D_MODEL, D_MODEL), D_MODEL ** -0.5),
        'norm2_w': gain(26, (DEPTH, D_MODEL)),
        'peer_w_q': nrm(27, (DEPTH, D_MODEL, PEER_HEADS * PEER_KEY_DIM), D_MODEL ** -0.5),
        'peer_sub_keys': nrm(28, (DEPTH, 2, N_KEYS, PEER_HALF), PEER_HALF ** -0.5),
        'peer_w_up': nrm(29, (DEPTH, N_EXPERTS, D_MODEL), D_MODEL ** -0.5),
        'peer_w_down': nrm(30, (DEPTH, N_EXPERTS, D_MODEL), PEER_HEADS ** -0.5),
    }


def reference(x_prompt, x_sample, cache_sb_k, cache_sb_v, cache_mem_k, cache_mem_v, state_conv, state_ssm,
              page_table, mem_prompt, norm1_w, w_in, conv_w, conv_b, dt_bias, a_log, d_skip, ssm_norm_w,
              w_ssm_out, w_sb_out, sb_bias, mem_norm_w, w_mem_kv, q_norm_w, k_norm_w, w_mem_out, w_out, norm2_w,
              peer_w_q, peer_sub_keys, peer_w_up, peer_w_down):
    bp = x_prompt.shape[0]
    bs = x_sample.shape[0]
    past_len = page_table.shape[1] * cache_sb_k.shape[2]
    y_prompt, y_sample = x_prompt, x_sample
    sbk_p, sbv_p, memk_p, memv_p, conv_p, ssm_p = [], [], [], [], [], []
    sbk_s, sbv_s, conv_s, ssm_s = [], [], [], []
    for l in range(DEPTH):
        lw = dict(norm1_w=norm1_w[l], w_in=w_in[l], conv_w=conv_w[l], conv_b=conv_b[l], dt_bias=dt_bias[l],
                  a_log=a_log[l], d_skip=d_skip[l], ssm_norm_w=ssm_norm_w[l], w_ssm_out=w_ssm_out[l],
                  w_sb_out=w_sb_out[l], sb_bias=sb_bias[l], mem_norm_w=mem_norm_w[l], w_mem_kv=w_mem_kv[l],
                  q_norm_w=q_norm_w[l], k_norm_w=k_norm_w[l], w_mem_out=w_mem_out[l], w_out=w_out[l],
                  norm2_w=norm2_w[l], peer_w_q=peer_w_q[l], peer_sub_keys=peer_sub_keys[l],
                  peer_w_up=peer_w_up[l], peer_w_down=peer_w_down[l])
        mk, mv = memory_kv(mem_prompt, lw)
        empty_kv = jnp.zeros((bp, 0, SB_HEADS, SB_HEAD_DIM), x_prompt.dtype)
        conv0 = jnp.zeros((bp, CONV_W - 1, CONV_DIM), x_prompt.dtype)
        ssm0 = jnp.zeros((bp, SSM_HEADS, SSM_HEAD_DIM, D_STATE), state_ssm.dtype)
        y_prompt, kp, vp, cp, sp = decoder_layer(y_prompt, mk, mv, empty_kv, empty_kv, conv0, ssm0, lw)
        sbk_p.append(kp)
        sbv_p.append(vp)
        memk_p.append(mk)
        memv_p.append(mv)
        conv_p.append(cp)
        ssm_p.append(sp)
        past_k = cache_sb_k[l][page_table].reshape(bs, past_len, SB_HEADS, SB_HEAD_DIM)
        past_v = cache_sb_v[l][page_table].reshape(bs, past_len, SB_HEADS, SB_HEAD_DIM)
        y_sample, ksm, vsm, csm, ssm_new = decoder_layer(y_sample, cache_mem_k[l], cache_mem_v[l], past_k, past_v,
                                                        state_conv[l], state_ssm[l], lw)
        sbk_s.append(ksm)
        sbv_s.append(vsm)
        conv_s.append(csm)
        ssm_s.append(ssm_new)
    return (y_prompt, y_sample, jnp.stack(sbk_p), jnp.stack(sbv_p), jnp.stack(memk_p), jnp.stack(memv_p),
            jnp.stack(conv_p), jnp.stack(ssm_p), jnp.stack(sbk_s), jnp.stack(sbv_s), jnp.stack(conv_s),
            jnp.stack(ssm_s))
```

```python
import functools

import jax
import jax.numpy as jnp
from jax import lax
from jax.experimental import pallas as pl
from jax.experimental.pallas import tpu as pltpu

F32 = jnp.float32
BF16 = jnp.bfloat16

D_MODEL = 1024
D_INNER = 2048
SSM_HEADS = 32
SSM_HEAD_DIM = 64
SSM_GROUPS = 4
GROUP_W = D_INNER // SSM_GROUPS
D_STATE = 128
CONV_W = 4
CONV_DIM = D_INNER + 2 * SSM_GROUPS * D_STATE
SB_HEADS = 16
SB_HEAD_DIM = 64
MEM_HEADS = 4
MEM_HEAD_DIM = 256
PEER_HEADS = 8
N_KEYS = 128
PEER_TOPK = 16
RMS_EPS = 1e-6
NEG = -1e30

LANES = 128
CHUNK = 128
VMEM_LIMIT = 56 * 1024 * 1024

Z0 = 0
XS0 = Z0 + D_INNER
B0 = XS0 + D_INNER
C0 = B0 + SSM_GROUPS * D_STATE
Q0 = C0 + SSM_GROUPS * D_STATE
K0 = Q0 + D_MODEL
V0 = K0 + D_MODEL
QM0 = V0 + D_MODEL
G0 = QM0 + D_MODEL
DTC0 = G0 + 3 * D_MODEL
DTX0 = DTC0 + SSM_GROUPS * LANES
N_IN = DTX0 + D_INNER


def _cparams(sem):
    return pltpu.CompilerParams(dimension_semantics=sem, vmem_limit_bytes=VMEM_LIMIT)


def _softplus(v):
    return jnp.maximum(v, 0.0) + jnp.log1p(jnp.exp(-jnp.abs(v)))


def _log_sigmoid(v):
    return jnp.minimum(v, 0.0) - jnp.log1p(jnp.exp(-jnp.abs(v)))


def _split_bf16(v):
    hi = v.astype(BF16)
    lo = (v - hi.astype(F32)).astype(BF16)
    return hi, lo


def _norm_matmul_kernel(x_ref, nw_ref, w_ref, o_ref, h_ref):
    @pl.when(pl.program_id(1) == 0)
    def _():
        x = x_ref[...]
        ms = jnp.mean(x * x, axis=-1, keepdims=True)
        h_ref[...] = (x * lax.rsqrt(ms + RMS_EPS) * nw_ref[...]).astype(BF16)

    o_ref[...] = jnp.dot(h_ref[...], w_ref[...], preferred_element_type=F32)


def norm_matmul(x, nw, w, tm, tn):
    t, d = x.shape
    n = w.shape[1]
    return pl.pallas_call(
        _norm_matmul_kernel,
        grid=(t // tm, n // tn),
        in_specs=[pl.BlockSpec((tm, d), lambda i, j: (i, 0)),
                  pl.BlockSpec((1, d), lambda i, j: (0, 0)),
                  pl.BlockSpec((d, tn), lambda i, j: (0, j))],
        out_specs=pl.BlockSpec((tm, tn), lambda i, j: (i, j)),
        out_shape=jax.ShapeDtypeStruct((t, n), F32),
        scratch_shapes=[pltpu.VMEM((tm, d), BF16)],
        compiler_params=_cparams(("parallel", "arbitrary")),
        name="norm_matmul",
    )(x, nw.reshape(1, d), w)


def _headnorm_kernel(x_ref, w_ref, o_ref):
    for h in range(MEM_HEADS):
        sl = slice(h * MEM_HEAD_DIM, (h + 1) * MEM_HEAD_DIM)
        x = x_ref[:, sl]
        ms = jnp.mean(x * x, axis=-1, keepdims=True)
        o_ref[:, sl] = x * lax.rsqrt(ms + RMS_EPS) * w_ref[...]


def headnorm(kv, w, tm):
    t = kv.shape[0]
    width = MEM_HEADS * MEM_HEAD_DIM
    return pl.pallas_call(
        _headnorm_kernel,
        grid=(t // tm,),
        in_specs=[pl.BlockSpec((tm, width), lambda i: (i, 0)),
                  pl.BlockSpec((1, MEM_HEAD_DIM), lambda i: (0, 0))],
        out_specs=pl.BlockSpec((tm, width), lambda i: (i, 0)),
        out_shape=jax.ShapeDtypeStruct((t, width), F32),
        compiler_params=_cparams(("parallel",)),
        name="headnorm",
    )(kv, w.reshape(1, MEM_HEAD_DIM))


def _cumsum_rows(x):
    n = x.shape[0]
    r = lax.broadcasted_iota(jnp.int32, (n, 1), 0)
    s = 1
    while s < n:
        x = x + jnp.where(r >= s, pltpu.roll(x, s, axis=0), 0.0)
        s *= 2
    return x


def _ssd_kernel(*refs, qr, nc, has_init):
    if has_init:
        (xs_ref, b_ref, c_ref, z_ref, dtc_ref, dtx_ref, prev_ref, cw_ref, hp_ref, hx_ref, s0_ref,
         y_ref, so_ref, buf_ref, tail_ref, st_ref, dtb_ref) = refs
    else:
        (xs_ref, b_ref, c_ref, z_ref, dtc_ref, dtx_ref, prev_ref, cw_ref, hp_ref, hx_ref,
         y_ref, so_ref, buf_ref, tail_ref, st_ref, dtb_ref) = refs
        s0_ref = None
    q = CHUNK
    cw_w = GROUP_W + 2 * D_STATE
    c = pl.program_id(2)

    @pl.when(c == 0)
    def _():
        tail_ref[...] = prev_ref[...]
        if has_init:
            st_ref[...] = s0_ref[...].reshape(GROUP_W, D_STATE).T
        else:
            st_ref[...] = jnp.zeros_like(st_ref)

    buf_ref[0:8, :] = tail_ref[...]
    if qr < q:
        buf_ref[8:, :] = jnp.zeros((q, cw_w), F32)
        dtb_ref[...] = jnp.zeros_like(dtb_ref)
    buf_ref[8:8 + qr, 0:GROUP_W] = xs_ref[...]
    buf_ref[8:8 + qr, GROUP_W:GROUP_W + D_STATE] = b_ref[...]
    buf_ref[8:8 + qr, GROUP_W + D_STATE:cw_w] = c_ref[...]
    dtb_ref[0:qr, 0:LANES] = dtc_ref[...]
    dtb_ref[0:qr, LANES:] = dtx_ref[...]
    if nc > 1:
        tail_ref[...] = buf_ref[q:q + 8, :]

    acc = jnp.broadcast_to(cw_ref[CONV_W:CONV_W + 1, :], (q, cw_w))
    for w in range(CONV_W):
        acc = acc + cw_ref[w:w + 1, :] * buf_ref[5 + w:5 + w + q, :]
    xbc = acc * jax.nn.sigmoid(acc)
    xs = xbc[:, 0:GROUP_W]
    bm = xbc[:, GROUP_W:GROUP_W + D_STATE]
    cm = xbc[:, GROUP_W + D_STATE:cw_w]

    rows = lax.broadcasted_iota(jnp.int32, (q, 1), 0)
    dtc = _softplus(dtb_ref[:, 0:LANES] + hp_ref[0:1, :])
    dtx = _softplus(dtb_ref[:, LANES:] + hx_ref[0:1, :])
    if qr < q:
        dtc = jnp.where(rows < qr, dtc, 0.0)
        dtx = jnp.where(rows < qr, dtx, 0.0)
    csc = _cumsum_rows(dtc * (-jnp.exp(hp_ref[1:2, :])))
    csx = _cumsum_rows(dtx * (-jnp.exp(hx_ref[1:2, :])))
    cst = csc.T
    cb = lax.dot_general(cm.astype(BF16), bm.astype(BF16), (((1,), (1,)), ((), ())),
                         preferred_element_type=F32)
    tri = rows >= lax.broadcasted_iota(jnp.int32, (1, q), 1)
    tot = csx[q - 1:q, :]
    xdt = xs * dtx
    wgt = xdt * jnp.exp(tot - csx)
    ecs = jnp.exp(csc)
    st = st_ref[...]
    half = lax.broadcasted_iota(jnp.int32, (1, LANES), 1) // SSM_HEAD_DIM
    ys = []
    for pr in range(GROUP_W // LANES):
        xp = xdt[:, pr * LANES:(pr + 1) * LANES]
        sp = st[:, pr * LANES:(pr + 1) * LANES]
        ya = jnp.zeros((q, LANES), F32)
        for hh in range(LANES // SSM_HEAD_DIM):
            r = pr * (LANES // SSM_HEAD_DIM) + hh
            seg = csc[:, r:r + 1] - cst[r:r + 1, :]
            m = (cb * jnp.exp(jnp.where(tri, seg, NEG))).astype(BF16)
            ce = (cm * ecs[:, r:r + 1]).astype(BF16)
            xm = jnp.where(half == hh, xp, 0.0).astype(BF16)
            sm = jnp.where(half == hh, sp, 0.0).astype(BF16)
            ya = ya + jnp.dot(m, xm, preferred_element_type=F32)
            ya = ya + jnp.dot(ce, sm, preferred_element_type=F32)
        ys.append(ya)
    y = jnp.concatenate(ys, axis=1) + hx_ref[2:3, :] * xs

    new_st = st * jnp.exp(tot) + jnp.dot(bm.T.astype(BF16), wgt.astype(BF16), preferred_element_type=F32)
    st_ref[...] = new_st

    @pl.when(c == nc - 1)
    def _():
        so_ref[...] = new_st.T.reshape(GROUP_W // SSM_HEAD_DIM, SSM_HEAD_DIM, D_STATE)

    z = z_ref[...]
    g = y[0:qr] * (z * jax.nn.sigmoid(z))
    ms = jnp.mean(g * g, axis=-1, keepdims=True)
    y_ref[...] = g * lax.rsqrt(ms + RMS_EPS) * hx_ref[3:4, :]


def ssd(yin, prev, cw, hp, hx, s0, nb, seq):
    if seq % CHUNK == 0:
        qr, nc = CHUNK, seq // CHUNK
    else:
        assert seq < CHUNK and seq % 8 == 0
        qr, nc = seq, 1
    has_init = s0 is not None
    cw_w = GROUP_W + 2 * D_STATE
    hpg = SSM_HEADS // SSM_GROUPS

    def row(b, g, c):
        return b * nc + c

    in_specs = [
        pl.BlockSpec((qr, GROUP_W), lambda b, g, c: (row(b, g, c), XS0 // GROUP_W + g)),
        pl.BlockSpec((qr, D_STATE), lambda b, g, c: (row(b, g, c), B0 // D_STATE + g)),
        pl.BlockSpec((qr, D_STATE), lambda b, g, c: (row(b, g, c), C0 // D_STATE + g)),
        pl.BlockSpec((qr, GROUP_W), lambda b, g, c: (row(b, g, c), Z0 // GROUP_W + g)),
        pl.BlockSpec((qr, LANES), lambda b, g, c: (row(b, g, c), DTC0 // LANES + g)),
        pl.BlockSpec((qr, GROUP_W), lambda b, g, c: (row(b, g, c), DTX0 // GROUP_W + g)),
        pl.BlockSpec((None, None, 8, cw_w), lambda b, g, c: (b, g, 0, 0)),
        pl.BlockSpec((None, 8, cw_w), lambda b, g, c: (g, 0, 0)),
        pl.BlockSpec((None, 8, LANES), lambda b, g, c: (g, 0, 0)),
        pl.BlockSpec((None, 8, GROUP_W), lambda b, g, c: (g, 0, 0)),
    ]
    args = [yin, yin, yin, yin, yin, yin, prev, cw, hp, hx]
    if has_init:
        in_specs.append(pl.BlockSpec((None, hpg, SSM_HEAD_DIM, D_STATE), lambda b, g, c: (b, g, 0, 0)))
        args.append(s0)
    return pl.pallas_call(
        functools.partial(_ssd_kernel, qr=qr, nc=nc, has_init=has_init),
        grid=(nb, SSM_GROUPS, nc),
        in_specs=in_specs,
        out_specs=[pl.BlockSpec((qr, GROUP_W), lambda b, g, c: (row(b, g, c), g)),
                   pl.BlockSpec((None, hpg, SSM_HEAD_DIM, D_STATE), lambda b, g, c: (b, g, 0, 0))],
        out_shape=[jax.ShapeDtypeStruct((nb * seq, D_INNER), F32),
                   jax.ShapeDtypeStruct((nb, SSM_HEADS, SSM_HEAD_DIM, D_STATE), F32)],
        scratch_shapes=[pltpu.VMEM((CHUNK + 8, cw_w), F32),
                        pltpu.VMEM((8, cw_w), F32),
                        pltpu.VMEM((D_STATE, GROUP_W), F32),
                        pltpu.VMEM((CHUNK, LANES + GROUP_W), F32)],
        compiler_params=_cparams(("parallel", "parallel", "arbitrary")),
        name="ssd",
    )(*args)


SB_KBLK = 256


def _sb_prompt_kernel(bias_ref, q_ref, k_ref, v_ref, o_ref):
    hp = pl.program_id(1)
    qi = pl.program_id(2)
    tq = CHUNK
    kb = SB_KBLK
    scale = SB_HEAD_DIM ** -0.5
    lane = lax.broadcasted_iota(jnp.int32, (1, LANES), 1)
    q = q_ref[...] * scale
    qs = [jnp.where(lane // SB_HEAD_DIM == hh, q, 0.0).astype(BF16) for hh in range(2)]
    bias = [bias_ref[2 * hp + hh] for hh in range(2)]
    upper = (lax.broadcasted_iota(jnp.int32, (kb, kb), 0) >
             lax.broadcasted_iota(jnp.int32, (kb, kb), 1)).astype(BF16)
    qpos = qi * tq + lax.broadcasted_iota(jnp.int32, (tq, 1), 0)
    kcol = lax.broadcasted_iota(jnp.int32, (1, kb), 1)

    def block(kblk, carry, masked):
        start = pl.multiple_of(kblk * kb, kb)
        k = k_ref[pl.ds(start, kb), :].astype(BF16)
        v = v_ref[pl.ds(start, kb), :]
        vis = (start + kcol) < qpos
        out = []
        for hh in range(2):
            run, acc = carry[hh]
            z = lax.dot_general(qs[hh], k, (((1,), (1,)), ((), ())), preferred_element_type=F32) + bias[hh]
            ls = _log_sigmoid(z)
            lk = ls - z
            if masked:
                lk = jnp.where(vis, lk, 0.0)
            hi, lo = _split_bf16(lk)
            later = (jnp.dot(hi, upper, preferred_element_type=F32) +
                     jnp.dot(lo, upper, preferred_element_type=F32))
            w = jnp.exp(ls + later + run)
            if masked:
                w = jnp.where(vis, w, 0.0)
            vm = jnp.where(lane // SB_HEAD_DIM == hh, v, 0.0).astype(BF16)
            acc = acc + jnp.dot(w.astype(BF16), vm, preferred_element_type=F32)
            run = run + jnp.sum(lk, axis=1, keepdims=True)
            out.append((run, acc))
        return tuple(out)

    zero = (jnp.zeros((tq, 1), F32), jnp.zeros((tq, LANES), F32))
    top = (qi * tq) // kb
    carry = block(top, (zero, zero), True)

    def body(i, carry):
        return block(top - 1 - i, carry, False)

    carry = lax.fori_loop(0, top, body, carry)
    o_ref[...] = carry[0][1] + carry[1][1]


def sb_prompt(yin, bias, nb, seq):
    assert seq % SB_KBLK == 0
    nq = seq // CHUNK
    return pl.pallas_call(
        _sb_prompt_kernel,
        grid=(nb, SB_HEADS // 2, nq),
        in_specs=[pl.BlockSpec(memory_space=pltpu.SMEM),
                  pl.BlockSpec((CHUNK, LANES), lambda b, h, i: (b * nq + i, Q0 // LANES + h)),
                  pl.BlockSpec((seq, LANES), lambda b, h, i: (b, K0 // LANES + h)),
                  pl.BlockSpec((seq, LANES), lambda b, h, i: (b, V0 // LANES + h))],
        out_specs=pl.BlockSpec((CHUNK, LANES), lambda b, h, i: (b * nq + i, h)),
        out_shape=jax.ShapeDtypeStruct((nb * seq, SB_HEADS * SB_HEAD_DIM), F32),
        compiler_params=_cparams(("parallel", "parallel", "arbitrary")),
        name="sb_prompt",
    )(bias, yin, yin, yin)


SB_PAGES_PER_STEP = 4


def _sb_sample_kernel(pt_ref, bias_ref, q_ref, kn_ref, vn_ref, *refs, npp, dq):
    krefs = refs[:npp]
    vrefs = refs[npp:2 * npp]
    o_ref = refs[2 * npp]
    qbt_ref, run_ref, acc_ref = refs[2 * npp + 1:]
    j = pl.program_id(1)
    width = SB_HEADS * SB_HEAD_DIM
    hq = SB_HEADS * dq
    page = CHUNK
    rowh = lax.broadcasted_iota(jnp.int32, (hq, 1), 0) // dq
    colh = lax.broadcasted_iota(jnp.int32, (1, width), 1) // SB_HEAD_DIM
    diag = rowh == colh
    srow = lax.broadcasted_iota(jnp.int32, (page, 1), 0)
    upper_t = (lax.broadcasted_iota(jnp.int32, (page, page), 1) >
               lax.broadcasted_iota(jnp.int32, (page, page), 0)).astype(BF16)

    def process(k, v, vis):
        zt = lax.dot_general(k.astype(BF16), qbt_ref[...], (((1,), (1,)), ((), ())),
                             preferred_element_type=F32) + bias_ref[...]
        ls = _log_sigmoid(zt)
        lk = ls - zt
        if vis is not None:
            lk = jnp.where(vis, lk, 0.0)
        hi, lo = _split_bf16(lk)
        later = (jnp.dot(upper_t, hi, preferred_element_type=F32) +
                 jnp.dot(upper_t, lo, preferred_element_type=F32))
        wt = jnp.exp(ls + later + run_ref[...])
        if vis is not None:
            wt = jnp.where(vis, wt, 0.0)
        acc_ref[...] += jnp.dot(wt.T.astype(BF16), v.astype(BF16), preferred_element_type=F32)
        run_ref[...] += jnp.sum(lk, axis=0, keepdims=True)

    @pl.when(j == 0)
    def _():
        qs = q_ref[...] * (SB_HEAD_DIM ** -0.5)
        qrep = jnp.concatenate([qs] * SB_HEADS, axis=0)
        qbt_ref[...] = jnp.where(diag, qrep, 0.0).astype(BF16)
        run_ref[...] = jnp.zeros_like(run_ref)
        acc_ref[...] = jnp.zeros_like(acc_ref)
        pad = jnp.zeros((page - dq, width), F32)
        kn = jnp.concatenate([kn_ref[...], pad], axis=0)
        vn = jnp.concatenate([vn_ref[...], pad], axis=0)
        qidx = lax.broadcasted_iota(jnp.int32, (1, hq), 1) % dq
        process(kn, vn, srow < qidx)

    for i in range(npp):
        process(krefs[i][...], vrefs[i][...], None)

    @pl.when(j == pl.num_programs(1) - 1)
    def _():
        sel = jnp.where(diag, acc_ref[...], 0.0)
        o = sel[0:dq]
        for h in range(1, SB_HEADS):
            o = o + sel[h * dq:(h + 1) * dq]
        o_ref[...] = o


def sb_sample(yin, bias, cache_k, cache_v, page_table, nb, dq):
    n_pool, page = cache_k.shape[0], cache_k.shape[1]
    assert page == CHUNK and dq == 8
    width = SB_HEADS * SB_HEAD_DIM
    npages = page_table.shape[1]
    npp = SB_PAGES_PER_STEP
    assert npages % npp == 0
    hq = SB_HEADS * dq
    ck = cache_k.reshape(n_pool, page, width)
    cv = cache_v.reshape(n_pool, page, width)
    bias_l = jnp.repeat(bias, dq).reshape(1, hq)

    def page_spec(i):
        return pl.BlockSpec((None, page, width),
                            lambda b, j, pt: (pt[b, npages - 1 - (j * npp + i)], 0, 0))

    grid_spec = pltpu.PrefetchScalarGridSpec(
        num_scalar_prefetch=1,
        grid=(nb, npages // npp),
        in_specs=[pl.BlockSpec((1, hq), lambda b, j, pt: (0, 0)),
                  pl.BlockSpec((dq, width), lambda b, j, pt: (b, Q0 // width)),
                  pl.BlockSpec((dq, width), lambda b, j, pt: (b, K0 // width)),
                  pl.BlockSpec((dq, width), lambda b, j, pt: (b, V0 // width))]
                 + [page_spec(i) for i in range(npp)] + [page_spec(i) for i in range(npp)],
        out_specs=pl.BlockSpec((dq, width), lambda b, j, pt: (b, 0)),
        scratch_shapes=[pltpu.VMEM((hq, width), BF16),
                        pltpu.VMEM((1, hq), F32),
                        pltpu.VMEM((hq, width), F32)],
    )
    return pl.pallas_call(
        functools.partial(_sb_sample_kernel, npp=npp, dq=dq),
        grid_spec=grid_spec,
        out_shape=jax.ShapeDtypeStruct((nb * dq, width), F32),
        compiler_params=_cparams(("parallel", "arbitrary")),
        name="sb_sample",
    )(page_table, bias_l, yin, yin, yin, *([ck] * npp), *([cv] * npp))


def _mem_attend_kernel(q_ref, k_ref, v_ref, w_ref, o_ref):
    for h in range(MEM_HEADS):
        sl = slice(h * MEM_HEAD_DIM, (h + 1) * MEM_HEAD_DIM)
        q = q_ref[:, sl]
        ms = jnp.mean(q * q, axis=-1, keepdims=True)
        qn = (q * lax.rsqrt(ms + RMS_EPS) * w_ref[...]).astype(BF16)
        s = lax.dot_general(qn, k_ref[:, sl].astype(BF16), (((1,), (1,)), ((), ())),
                            preferred_element_type=F32) * (MEM_HEAD_DIM ** -0.5)
        e = jnp.exp(s - jnp.max(s, axis=-1, keepdims=True))
        p = e / jnp.sum(e, axis=-1, keepdims=True)
        o_ref[:, sl] = jnp.dot(p.astype(BF16), v_ref[:, sl].astype(BF16), preferred_element_type=F32)


def mem_attend(yin, mem_k, mem_v, qnw, nb, seq, tq):
    width = MEM_HEADS * MEM_HEAD_DIM
    mt = mem_k.shape[1]
    nq = seq // tq
    return pl.pallas_call(
        _mem_attend_kernel,
        grid=(nb, nq),
        in_specs=[pl.BlockSpec((tq, width), lambda b, i: (b * nq + i, QM0 // width)),
                  pl.BlockSpec((None, mt, width), lambda b, i: (b, 0, 0)),
                  pl.BlockSpec((None, mt, width), lambda b, i: (b, 0, 0)),
                  pl.BlockSpec((1, MEM_HEAD_DIM), lambda b, i: (0, 0))],
        out_specs=pl.BlockSpec((tq, width), lambda b, i: (b * nq + i, 0)),
        out_shape=jax.ShapeDtypeStruct((nb * seq, width), F32),
        compiler_params=_cparams(("parallel", "arbitrary")),
        name="mem_attend",
    )(yin, mem_k, mem_v, qnw.reshape(1, MEM_HEAD_DIM))


def _merge_kernel(x_ref, y_ref, sb_ref, mm_ref, g1_ref, g2_ref, g3_ref, w1_ref, w2_ref, w3_ref, wo_ref, o_ref):
    b1 = jnp.dot(y_ref[...].astype(BF16), w1_ref[...], preferred_element_type=F32)
    b2 = jnp.dot(sb_ref[...].astype(BF16), w2_ref[...], preferred_element_type=F32)
    b3 = jnp.dot(mm_ref[...].astype(BF16), w3_ref[...], preferred_element_type=F32)
    merged = (jax.nn.sigmoid(g1_ref[...]) * b1 + jax.nn.sigmoid(g2_ref[...]) * b2 +
              jax.nn.sigmoid(g3_ref[...]) * b3)
    o_ref[...] = x_ref[...] + jnp.dot(merged.astype(BF16), wo_ref[...], preferred_element_type=F32)


def merge(x, yin, y_ssm, o_sb, o_mem, w1, w2, w3, wo, tm):
    t, d = x.shape

    def tok(width, col=0):
        return pl.BlockSpec((tm, width), lambda i: (i, col))

    def full(w):
        return pl.BlockSpec(w.shape, lambda i: (0, 0))

    return pl.pallas_call(
        _merge_kernel,
        grid=(t // tm,),
        in_specs=[tok(d), tok(D_INNER), tok(d), tok(d),
                  tok(d, G0 // d), tok(d, G0 // d + 1), tok(d, G0 // d + 2),
                  full(w1), full(w2), full(w3), full(wo)],
        out_specs=tok(d),
        out_shape=jax.ShapeDtypeStruct((t, d), F32),
        compiler_params=_cparams(("parallel",)),
        name="merge",
    )(x, y_ssm, o_sb, o_mem, yin, yin, yin, w1, w2, w3, wo)


PEER_TB = 256


def _top_values(s, n):
    out = []
    for _ in range(n):
        m = jnp.max(s, axis=0, keepdims=True)
        out.append(m)
        s = jnp.where(s == m, NEG, s)
    return out


def _peer_stats_kernel(q_ref, keys_ref, s0_ref, s1_ref, st_ref):
    tb = q_ref.shape[0]
    taus, a0s, b0s, izs = [], [], [], []
    for h in range(PEER_HEADS):
        tops = []
        for c in range(2):
            col = (h * 2 + c) * N_KEYS
            s = lax.dot_general(keys_ref[c], q_ref[:, col:col + N_KEYS].astype(BF16),
                                (((1,), (1,)), ((), ())), preferred_element_type=F32)
            (s0_ref if c == 0 else s1_ref)[h] = s
            tops.append(_top_values(s, PEER_TOPK))
        b = jnp.concatenate(tops[1], axis=0)
        cand = jnp.concatenate([a + b for a in tops[0]], axis=0)
        best = _top_values(cand, PEER_TOPK)
        zsum = jnp.zeros((1, tb), F32)
        for m in best:
            zsum = zsum + jnp.exp(m - best[0])
        taus.append(best[-1])
        a0s.append(tops[0][0])
        b0s.append(tops[1][0])
        izs.append(1.0 / zsum)
    st_ref[0] = jnp.concatenate(taus, axis=0)
    st_ref[1] = jnp.concatenate(a0s, axis=0)
    st_ref[2] = jnp.concatenate(b0s, axis=0)
    st_ref[3] = jnp.concatenate(izs, axis=0)


def peer_stats(qp, keys):
    t = qp.shape[0]
    tb = PEER_TB
    return pl.pallas_call(
        _peer_stats_kernel,
        grid=(t // tb,),
        in_specs=[pl.BlockSpec((tb, qp.shape[1]), lambda i: (i, 0)),
                  pl.BlockSpec(keys.shape, lambda i: (0, 0, 0))],
        out_specs=[pl.BlockSpec((PEER_HEADS, N_KEYS, tb), lambda i: (0, 0, i)),
                   pl.BlockSpec((PEER_HEADS, N_KEYS, tb), lambda i: (0, 0, i)),
                   pl.BlockSpec((4, PEER_HEADS, tb), lambda i: (0, 0, i))],
        out_shape=[jax.ShapeDtypeStruct((PEER_HEADS, N_KEYS, t), F32),
                   jax.ShapeDtypeStruct((PEER_HEADS, N_KEYS, t), F32),
                   jax.ShapeDtypeStruct((4, PEER_HEADS, t), F32)],
        compiler_params=_cparams(("parallel",)),
        name="peer_stats",
    )(qp, keys)


PEER_TBK = 512
PEER_EB = 512


def _peer_expert_kernel(x_ref, nw_ref, wu_ref, wd_ref, s0_ref, s1_ref, st_ref, o_ref,
                        xt_ref, acc_ref, p0_ref, p1_ref):
    e = pl.program_id(1)
    ne = pl.num_programs(1)
    nsub = PEER_EB // N_KEYS

    @pl.when(e == 0)
    def _():
        x = x_ref[...]
        ms = jnp.mean(x * x, axis=-1, keepdims=True)
        xt_ref[...] = (x * lax.rsqrt(ms + RMS_EPS) * nw_ref[...]).T.astype(BF16)
        acc_ref[...] = jnp.zeros_like(acc_ref)
        for h in range(PEER_HEADS):
            p0_ref[h] = jnp.exp(s0_ref[h] - st_ref[1, h:h + 1, :])
            p1_ref[h] = jnp.exp(s1_ref[h] - st_ref[2, h:h + 1, :]) * st_ref[3, h:h + 1, :]

    act = jnp.dot(wu_ref[...], xt_ref[...], preferred_element_type=F32)
    coefs = []
    for ii in range(nsub):
        i = e * nsub + ii
        gate = jnp.zeros((N_KEYS, act.shape[1]), F32)
        for h in range(PEER_HEADS):
            score = s0_ref[h, pl.ds(i, 1), :] + s1_ref[h]
            pr = p0_ref[h, pl.ds(i, 1), :] * p1_ref[h]
            gate = gate + jnp.where(score >= st_ref[0, h:h + 1, :], pr, 0.0)
        a = act[ii * N_KEYS:(ii + 1) * N_KEYS]
        gelu = 0.5 * a * (1.0 + lax.erf(a * (2.0 ** -0.5)))
        coefs.append((gate * gelu).astype(BF16))
    coef = jnp.concatenate(coefs, axis=0)
    acc_ref[...] += jnp.dot(wd_ref[...], coef, preferred_element_type=F32)

    @pl.when(e == ne - 1)
    def _():
        o_ref[...] = x_ref[...] + acc_ref[...].T


def peer_experts(x, nw, wu, wdt, s0t, s1t, st):
    t, d = x.shape
    tb, eb = PEER_TBK, PEER_EB
    ne = wu.shape[0] // eb
    return pl.pallas_call(
        _peer_expert_kernel,
        grid=(t // tb, ne),
        in_specs=[pl.BlockSpec((tb, d), lambda i, e: (i, 0)),
                  pl.BlockSpec((1, d), lambda i, e: (0, 0)),
                  pl.BlockSpec((eb, d), lambda i, e: (e, 0)),
                  pl.BlockSpec((d, eb), lambda i, e: (0, e)),
                  pl.BlockSpec((PEER_HEADS, N_KEYS, tb), lambda i, e: (0, 0, i)),
                  pl.BlockSpec((PEER_HEADS, N_KEYS, tb), lambda i, e: (0, 0, i)),
                  pl.BlockSpec((4, PEER_HEADS, tb), lambda i, e: (0, 0, i))],
        out_specs=pl.BlockSpec((tb, d), lambda i, e: (i, 0)),
        out_shape=jax.ShapeDtypeStruct((t, d), F32),
        scratch_shapes=[pltpu.VMEM((d, tb), BF16),
                        pltpu.VMEM((d, tb), F32),
                        pltpu.VMEM((PEER_HEADS, N_KEYS, tb), F32),
                        pltpu.VMEM((PEER_HEADS, N_KEYS, tb), F32)],
        compiler_params=_cparams(("parallel", "arbitrary")),
        name="peer_experts",
    )(x, nw.reshape(1, d), wu, wdt, s0t, s1t, st)


def _pad_rows(x, mult):
    t = x.shape[0]
    pad = (-t) % mult
    return x if pad == 0 else jnp.pad(x, ((0, pad), (0, 0)))


def _pick(t, pref):
    return pref if t % pref == 0 else t


def _peer(x1, wts):
    t = x1.shape[0]
    xp = _pad_rows(x1, PEER_TBK)
    qp = norm_matmul(xp, wts["norm2_w"], wts["peer_w_q"], _pick(xp.shape[0], 1024), 512)
    s0t, s1t, st = peer_stats(qp, wts["peer_keys"])
    out = peer_experts(xp, wts["norm2_w"], wts["peer_w_up"], wts["peer_w_down_t"], s0t, s1t, st)
    return out[:t]


def _layer(x, mem_k, mem_v, conv_prev, ssm_prev, wts, sb_fn):
    nb, seq, d = x.shape
    t = nb * seq
    xf = x.reshape(t, d)
    yin = norm_matmul(xf, wts["norm1_w"], wts["w_in"], _pick(t, 1024), 512)
    y_ssm, ssm_new = ssd(yin, conv_prev, wts["conv_wg"], wts["ssd_hp"], wts["ssd_hx"], ssm_prev, nb, seq)
    o_sb = sb_fn(yin)
    o_mem = mem_attend(yin, mem_k, mem_v, wts["q_norm_w"], nb, seq, _pick(seq, 512))
    x1 = merge(xf, yin, y_ssm, o_sb, o_mem, wts["w_ssm_out"], wts["w_sb_out"], wts["w_mem_out"], wts["w_out"],
               _pick(t, 256))
    y = _peer(x1, wts)
    y3 = yin.reshape(nb, seq, N_IN)
    k_new = y3[:, :, K0:K0 + d].reshape(nb, seq, SB_HEADS, SB_HEAD_DIM)
    v_new = y3[:, :, V0:V0 + d].reshape(nb, seq, SB_HEADS, SB_HEAD_DIM)
    xbc_raw = y3[:, :, XS0:XS0 + CONV_DIM]
    return y.reshape(nb, seq, d), k_new, v_new, xbc_raw, ssm_new


def _group_cols(a):
    lead = a.shape[:-1]
    xs = a[..., :D_INNER].reshape(lead + (SSM_GROUPS, GROUP_W))
    bm = a[..., D_INNER:D_INNER + SSM_GROUPS * D_STATE].reshape(lead + (SSM_GROUPS, D_STATE))
    cm = a[..., D_INNER + SSM_GROUPS * D_STATE:].reshape(lead + (SSM_GROUPS, D_STATE))
    return jnp.concatenate([xs, bm, cm], axis=-1)


def _prev_rows(prev):
    g = jnp.moveaxis(_group_cols(prev), 2, 1)
    return jnp.pad(g, ((0, 0), (0, 0), (8 - (CONV_W - 1), 0), (0, 0)))


def _prep_weights(l, norm1_w, w_in, conv_w, conv_b, dt_bias, a_log, d_skip, ssm_norm_w, w_ssm_out, w_sb_out,
                  sb_bias, mem_norm_w, w_mem_kv, q_norm_w, k_norm_w, w_mem_out, w_out, norm2_w, peer_w_q,
                  peer_sub_keys, peer_w_up, peer_w_down):
    hpg = SSM_HEADS // SSM_GROUPS
    w = w_in[l]
    dt_cols = w[:, D_INNER + CONV_DIM:D_INNER + CONV_DIM + SSM_HEADS]
    rest = w[:, D_INNER + CONV_DIM + SSM_HEADS:]
    dtc = jnp.pad(dt_cols.reshape(D_MODEL, SSM_GROUPS, hpg), ((0, 0), (0, 0), (0, LANES - hpg)))
    w_fused = jnp.concatenate([w[:, :D_INNER + CONV_DIM], rest, dtc.reshape(D_MODEL, SSM_GROUPS * LANES),
                               jnp.repeat(dt_cols, SSM_HEAD_DIM, axis=1)], axis=1).astype(BF16)
    assert w_fused.shape[1] == N_IN

    def compact(v):
        return jnp.pad(v.reshape(SSM_GROUPS, hpg), ((0, 0), (0, LANES - hpg)))

    def expand(v):
        return jnp.repeat(v, SSM_HEAD_DIM).reshape(SSM_GROUPS, GROUP_W)

    zc = jnp.zeros((SSM_GROUPS, LANES), F32)
    zx = jnp.zeros((SSM_GROUPS, GROUP_W), F32)
    ssd_hp = jnp.stack([compact(dt_bias[l]), compact(a_log[l])] + [zc] * 6, axis=1)
    ssd_hx = jnp.stack([expand(dt_bias[l]), expand(a_log[l]), expand(d_skip[l]),
                        ssm_norm_w[l].reshape(SSM_GROUPS, GROUP_W)] + [zx] * 4, axis=1)
    conv_rows = jnp.concatenate([conv_w[l], conv_b[l][None]], axis=0)
    conv_wg = jnp.pad(jnp.moveaxis(_group_cols(conv_rows), 1, 0), ((0, 0), (0, 8 - (CONV_W + 1)), (0, 0)))
    return dict(
        norm1_w=norm1_w[l], w_in=w_fused, conv_wg=conv_wg, ssd_hp=ssd_hp, ssd_hx=ssd_hx,
        w_ssm_out=w_ssm_out[l].astype(BF16), w_sb_out=w_sb_out[l].astype(BF16), sb_bias=sb_bias[l],
        mem_norm_w=mem_norm_w[l], w_mem_kv=w_mem_kv[l].astype(BF16), q_norm_w=q_norm_w[l],
        k_norm_w=k_norm_w[l], w_mem_out=w_mem_out[l].astype(BF16), w_out=w_out[l].astype(BF16),
        norm2_w=norm2_w[l], peer_w_q=peer_w_q[l].astype(BF16), peer_keys=peer_sub_keys[l].astype(BF16),
        peer_w_up=peer_w_up[l].astype(BF16), peer_w_down_t=peer_w_down[l].T.astype(BF16))


def kernel(x_prompt, x_sample, cache_sb_k, cache_sb_v, cache_mem_k, cache_mem_v, state_conv, state_ssm, page_table, mem_prompt, norm1_w, w_in, conv_w, conv_b, dt_bias, a_log, d_skip, ssm_norm_w, w_ssm_out, w_sb_out, sb_bias, mem_norm_w, w_mem_kv, q_norm_w, k_norm_w, w_mem_out, w_out, norm2_w, peer_w_q, peer_sub_keys, peer_w_up, peer_w_down):
    bp, seq_p, d = x_prompt.shape
    bs, seq_s, _ = x_sample.shape
    depth = w_in.shape[0]
    mt = mem_prompt.shape[1]
    mem_w = MEM_HEADS * MEM_HEAD_DIM
    y_p, y_s = x_prompt, x_sample
    outs = [[] for _ in range(10)]
    for l in range(depth):
        wts = _prep_weights(l, norm1_w, w_in, conv_w, conv_b, dt_bias, a_log, d_skip, ssm_norm_w, w_ssm_out,
                            w_sb_out, sb_bias, mem_norm_w, w_mem_kv, q_norm_w, k_norm_w, w_mem_out, w_out,
                            norm2_w, peer_w_q, peer_sub_keys, peer_w_up, peer_w_down)
        kv = norm_matmul(mem_prompt.reshape(bp * mt, d), wts["mem_norm_w"], wts["w_mem_kv"],
                         _pick(bp * mt, 1024), 512)
        mk = headnorm(kv, wts["k_norm_w"], _pick(bp * mt, 512)).reshape(bp, mt, mem_w)
        mv = kv[:, mem_w:].reshape(bp, mt, mem_w)
        prev0 = jnp.zeros((bp, SSM_GROUPS, 8, GROUP_W + 2 * D_STATE), F32)
        y_p, kp, vp, xbc_p, sp = _layer(
            y_p, mk, mv, prev0, None, wts,
            lambda yin: sb_prompt(yin, wts["sb_bias"], bp, seq_p))
        y_s, ks, vs, xbc_s, ss = _layer(
            y_s, cache_mem_k[l].reshape(bs, mt, mem_w), cache_mem_v[l].reshape(bs, mt, mem_w),
            _prev_rows(state_conv[l]), state_ssm[l], wts,
            lambda yin: sb_sample(yin, wts["sb_bias"], cache_sb_k[l], cache_sb_v[l], page_table, bs, seq_s))
        conv_s = jnp.concatenate([state_conv[l], xbc_s], axis=1)[:, -(CONV_W - 1):]
        for lst, v in zip(outs, [kp, vp, mk.reshape(bp, mt, MEM_HEADS, MEM_HEAD_DIM),
                                 mv.reshape(bp, mt, MEM_HEADS, MEM_HEAD_DIM), xbc_p[:, -(CONV_W - 1):], sp,
                                 ks, vs, conv_s, ss]):
            lst.append(v)
    return (y_p, y_s) + tuple(jnp.stack(o) for o in outs)
```

```python
import functools

import jax
import jax.numpy as jnp
from jax import lax
from jax.experimental import pallas as pl
from jax.experimental.pallas import tpu as pltpu

F32 = jnp.float32
BF16 = jnp.bfloat16

D_MODEL = 1024
D_INNER = 2048
SSM_HEADS = 32
SSM_HEAD_DIM = 64
SSM_GROUPS = 4
GROUP_W = D_INNER // SSM_GROUPS
D_STATE = 128
CONV_W = 4
CONV_DIM = D_INNER + 2 * SSM_GROUPS * D_STATE
SB_HEADS = 16
SB_HEAD_DIM = 64
MEM_HEADS = 4
MEM_HEAD_DIM = 256
PEER_HEADS = 8
N_KEYS = 128
PEER_TOPK = 16
RMS_EPS = 1e-6
NEG = -1e30

LANES = 128
CHUNK = 128
VMEM_LIMIT = 56 * 1024 * 1024

Z0 = 0
XS0 = Z0 + D_INNER
B0 = XS0 + D_INNER
C0 = B0 + SSM_GROUPS * D_STATE
Q0 = C0 + SSM_GROUPS * D_STATE
K0 = Q0 + D_MODEL
V0 = K0 + D_MODEL
QM0 = V0 + D_MODEL
G0 = QM0 + D_MODEL
DTC0 = G0 + 3 * D_MODEL
DTX0 = DTC0 + SSM_GROUPS * LANES
N_IN = DTX0 + D_INNER


def _cparams(sem):
    return pltpu.CompilerParams(dimension_semantics=sem, vmem_limit_bytes=VMEM_LIMIT)


def _softplus(v):
    return jnp.maximum(v, 0.0) + jnp.log(1.0 + jnp.exp(-jnp.abs(v)))


def _log_sigmoid(v):
    return jnp.minimum(v, 0.0) - jnp.log(1.0 + jnp.exp(-jnp.abs(v)))


def _split_bf16(v):
    hi = v.astype(BF16)
    lo = (v - hi.astype(F32)).astype(BF16)
    return hi, lo


def _norm_matmul_kernel(x_ref, nw_ref, w_ref, o_ref, h_ref):
    @pl.when(pl.program_id(1) == 0)
    def _():
        x = x_ref[...]
        ms = jnp.mean(x * x, axis=-1, keepdims=True)
        h_ref[...] = (x * lax.rsqrt(ms + RMS_EPS) * nw_ref[...]).astype(BF16)

    o_ref[...] = jnp.dot(h_ref[...], w_ref[...], preferred_element_type=F32)


def norm_matmul(x, nw, w, tm, tn):
    t, d = x.shape
    n = w.shape[1]
    return pl.pallas_call(
        _norm_matmul_kernel,
        grid=(t // tm, n // tn),
        in_specs=[pl.BlockSpec((tm, d), lambda i, j: (i, 0)),
                  pl.BlockSpec((1, d), lambda i, j: (0, 0)),
                  pl.BlockSpec((d, tn), lambda i, j: (0, j))],
        out_specs=pl.BlockSpec((tm, tn), lambda i, j: (i, j)),
        out_shape=jax.ShapeDtypeStruct((t, n), F32),
        scratch_shapes=[pltpu.VMEM((tm, d), BF16)],
        compiler_params=_cparams(("parallel", "arbitrary")),
        name="norm_matmul",
    )(x, nw.reshape(1, d), w)


def _headnorm_kernel(x_ref, w_ref, o_ref):
    for h in range(MEM_HEADS):
        sl = slice(h * MEM_HEAD_DIM, (h + 1) * MEM_HEAD_DIM)
        x = x_ref[:, sl]
        ms = jnp.mean(x * x, axis=-1, keepdims=True)
        o_ref[:, sl] = x * lax.rsqrt(ms + RMS_EPS) * w_ref[...]


def headnorm(kv, w, tm):
    t = kv.shape[0]
    width = MEM_HEADS * MEM_HEAD_DIM
    return pl.pallas_call(
        _headnorm_kernel,
        grid=(t // tm,),
        in_specs=[pl.BlockSpec((tm, width), lambda i: (i, 0)),
                  pl.BlockSpec((1, MEM_HEAD_DIM), lambda i: (0, 0))],
        out_specs=pl.BlockSpec((tm, width), lambda i: (i, 0)),
        out_shape=jax.ShapeDtypeStruct((t, width), F32),
        compiler_params=_cparams(("parallel",)),
        name="headnorm",
    )(kv, w.reshape(1, MEM_HEAD_DIM))


def _cumsum_rows(x):
    n = x.shape[0]
    r = lax.broadcasted_iota(jnp.int32, (n, 1), 0)
    s = 1
    while s < n:
        x = x + jnp.where(r >= s, pltpu.roll(x, s, axis=0), 0.0)
        s *= 2
    return x


def _ssd_kernel(*refs, qr, nc, has_init):
    if has_init:
        (xs_ref, b_ref, c_ref, z_ref, dtc_ref, dtx_ref, prev_ref, cw_ref, hp_ref, hx_ref, s0_ref,
         y_ref, so_ref, buf_ref, tail_ref, st_ref, dtb_ref) = refs
    else:
        (xs_ref, b_ref, c_ref, z_ref, dtc_ref, dtx_ref, prev_ref, cw_ref, hp_ref, hx_ref,
         y_ref, so_ref, buf_ref, tail_ref, st_ref, dtb_ref) = refs
        s0_ref = None
    q = CHUNK
    cw_w = GROUP_W + 2 * D_STATE
    c = pl.program_id(2)

    @pl.when(c == 0)
    def _():
        tail_ref[...] = prev_ref[...]
        if has_init:
            st_ref[...] = s0_ref[...].reshape(GROUP_W, D_STATE).T
        else:
            st_ref[...] = jnp.zeros_like(st_ref)

    buf_ref[0:8, :] = tail_ref[...]
    if qr < q:
        buf_ref[8:, :] = jnp.zeros((q, cw_w), F32)
        dtb_ref[...] = jnp.zeros_like(dtb_ref)
    buf_ref[8:8 + qr, 0:GROUP_W] = xs_ref[...]
    buf_ref[8:8 + qr, GROUP_W:GROUP_W + D_STATE] = b_ref[...]
    buf_ref[8:8 + qr, GROUP_W + D_STATE:cw_w] = c_ref[...]
    dtb_ref[0:qr, 0:LANES] = dtc_ref[...]
    dtb_ref[0:qr, LANES:] = dtx_ref[...]
    if nc > 1:
        tail_ref[...] = buf_ref[q:q + 8, :]

    acc = jnp.broadcast_to(cw_ref[CONV_W:CONV_W + 1, :], (q, cw_w))
    for w in range(CONV_W):
        acc = acc + cw_ref[w:w + 1, :] * buf_ref[5 + w:5 + w + q, :]
    xbc = acc * jax.nn.sigmoid(acc)
    xs = xbc[:, 0:GROUP_W]
    bm = xbc[:, GROUP_W:GROUP_W + D_STATE]
    cm = xbc[:, GROUP_W + D_STATE:cw_w]

    rows = lax.broadcasted_iota(jnp.int32, (q, 1), 0)
    dtc = _softplus(dtb_ref[:, 0:LANES] + hp_ref[0:1, :])
    dtx = _softplus(dtb_ref[:, LANES:] + hx_ref[0:1, :])
    if qr < q:
        dtc = jnp.where(rows < qr, dtc, 0.0)
        dtx = jnp.where(rows < qr, dtx, 0.0)
    csc = _cumsum_rows(dtc * (-jnp.exp(hp_ref[1:2, :])))
    csx = _cumsum_rows(dtx * (-jnp.exp(hx_ref[1:2, :])))
    cst = csc.T
    cb = lax.dot_general(cm.astype(BF16), bm.astype(BF16), (((1,), (1,)), ((), ())),
                         preferred_element_type=F32)
    tri = rows >= lax.broadcasted_iota(jnp.int32, (1, q), 1)
    tot = csx[q - 1:q, :]
    xdt = xs * dtx
    wgt = xdt * jnp.exp(tot - csx)
    ecs = jnp.exp(csc)
    st = st_ref[...]
    half = lax.broadcasted_iota(jnp.int32, (1, LANES), 1) // SSM_HEAD_DIM
    ys = []
    for pr in range(GROUP_W // LANES):
        xp = xdt[:, pr * LANES:(pr + 1) * LANES]
        sp = st[:, pr * LANES:(pr + 1) * LANES]
        ya = jnp.zeros((q, LANES), F32)
        for hh in range(LANES // SSM_HEAD_DIM):
            r = pr * (LANES // SSM_HEAD_DIM) + hh
            seg = csc[:, r:r + 1] - cst[r:r + 1, :]
            m = (cb * jnp.exp(jnp.where(tri, seg, NEG))).astype(BF16)
            ce = (cm * ecs[:, r:r + 1]).astype(BF16)
            xm = jnp.where(half == hh, xp, 0.0).astype(BF16)
            sm = jnp.where(half == hh, sp, 0.0).astype(BF16)
            ya = ya + jnp.dot(m, xm, preferred_element_type=F32)
            ya = ya + jnp.dot(ce, sm, preferred_element_type=F32)
        ys.append(ya)
    y = jnp.concatenate(ys, axis=1) + hx_ref[2:3, :] * xs

    new_st = st * jnp.exp(tot) + jnp.dot(bm.T.astype(BF16), wgt.astype(BF16), preferred_element_type=F32)
    st_ref[...] = new_st

    @pl.when(c == nc - 1)
    def _():
        so_ref[...] = new_st.T.reshape(GROUP_W // SSM_HEAD_DIM, SSM_HEAD_DIM, D_STATE)

    z = z_ref[...]
    g = y[0:qr] * (z * jax.nn.sigmoid(z))
    ms = jnp.mean(g * g, axis=-1, keepdims=True)
    y_ref[...] = g * lax.rsqrt(ms + RMS_EPS) * hx_ref[3:4, :]


def ssd(yin, prev, cw, hp, hx, s0, nb, seq):
    if seq % CHUNK == 0:
        qr, nc = CHUNK, seq // CHUNK
    else:
        assert seq < CHUNK and seq % 8 == 0
        qr, nc = seq, 1
    has_init = s0 is not None
    cw_w = GROUP_W + 2 * D_STATE
    hpg = SSM_HEADS // SSM_GROUPS

    def row(b, g, c):
        return b * nc + c

    in_specs = [
        pl.BlockSpec((qr, GROUP_W), lambda b, g, c: (row(b, g, c), XS0 // GROUP_W + g)),
        pl.BlockSpec((qr, D_STATE), lambda b, g, c: (row(b, g, c), B0 // D_STATE + g)),
        pl.BlockSpec((qr, D_STATE), lambda b, g, c: (row(b, g, c), C0 // D_STATE + g)),
        pl.BlockSpec((qr, GROUP_W), lambda b, g, c: (row(b, g, c), Z0 // GROUP_W + g)),
        pl.BlockSpec((qr, LANES), lambda b, g, c: (row(b, g, c), DTC0 // LANES + g)),
        pl.BlockSpec((qr, GROUP_W), lambda b, g, c: (row(b, g, c), DTX0 // GROUP_W + g)),
        pl.BlockSpec((None, None, 8, cw_w), lambda b, g, c: (b, g, 0, 0)),
        pl.BlockSpec((None, 8, cw_w), lambda b, g, c: (g, 0, 0)),
        pl.BlockSpec((None, 8, LANES), lambda b, g, c: (g, 0, 0)),
        pl.BlockSpec((None, 8, GROUP_W), lambda b, g, c: (g, 0, 0)),
    ]
    args = [yin, yin, yin, yin, yin, yin, prev, cw, hp, hx]
    if has_init:
        in_specs.append(pl.BlockSpec((None, hpg, SSM_HEAD_DIM, D_STATE), lambda b, g, c: (b, g, 0, 0)))
        args.append(s0)
    return pl.pallas_call(
        functools.partial(_ssd_kernel, qr=qr, nc=nc, has_init=has_init),
        grid=(nb, SSM_GROUPS, nc),
        in_specs=in_specs,
        out_specs=[pl.BlockSpec((qr, GROUP_W), lambda b, g, c: (row(b, g, c), g)),
                   pl.BlockSpec((None, hpg, SSM_HEAD_DIM, D_STATE), lambda b, g, c: (b, g, 0, 0))],
        out_shape=[jax.ShapeDtypeStruct((nb * seq, D_INNER), F32),
                   jax.ShapeDtypeStruct((nb, SSM_HEADS, SSM_HEAD_DIM, D_STATE), F32)],
        scratch_shapes=[pltpu.VMEM((CHUNK + 8, cw_w), F32),
                        pltpu.VMEM((8, cw_w), F32),
                        pltpu.VMEM((D_STATE, GROUP_W), F32),
                        pltpu.VMEM((CHUNK, LANES + GROUP_W), F32)],
        compiler_params=_cparams(("parallel", "parallel", "arbitrary")),
        name="ssd",
    )(*args)


SB_KBLK = 256
SB_QBLK = 256


def _sb_prompt_kernel(bias_ref, q_ref, k_ref, v_ref, o_ref):
    hp = pl.program_id(1)
    qi = pl.program_id(2)
    tq = SB_QBLK
    kb = SB_KBLK
    scale = SB_HEAD_DIM ** -0.5
    lane = lax.broadcasted_iota(jnp.int32, (1, LANES), 1)
    q = q_ref[...] * scale
    qs = [jnp.where(lane // SB_HEAD_DIM == hh, q, 0.0).astype(BF16) for hh in range(2)]
    bias = [bias_ref[2 * hp + hh] for hh in range(2)]
    upper = (lax.broadcasted_iota(jnp.int32, (kb, kb), 0) >
             lax.broadcasted_iota(jnp.int32, (kb, kb), 1)).astype(BF16)
    qpos = qi * tq + lax.broadcasted_iota(jnp.int32, (tq, 1), 0)
    kcol = lax.broadcasted_iota(jnp.int32, (1, kb), 1)

    def block(kblk, carry, masked):
        start = pl.multiple_of(kblk * kb, kb)
        k = k_ref[pl.ds(start, kb), :].astype(BF16)
        v = v_ref[pl.ds(start, kb), :]
        vis = (start + kcol) < qpos
        out = []
        for hh in range(2):
            run, acc = carry[hh]
            z = lax.dot_general(qs[hh], k, (((1,), (1,)), ((), ())), preferred_element_type=F32) + bias[hh]
            ls = _log_sigmoid(z)
            lk = ls - z
            if masked:
                lk = jnp.where(vis, lk, 0.0)
            hi, lo = _split_bf16(lk)
            later = (jnp.dot(hi, upper, preferred_element_type=F32) +
                     jnp.dot(lo, upper, preferred_element_type=F32))
            w = jnp.exp(ls + later + run)
            if masked:
                w = jnp.where(vis, w, 0.0)
            vm = jnp.where(lane // SB_HEAD_DIM == hh, v, 0.0).astype(BF16)
            acc = acc + jnp.dot(w.astype(BF16), vm, preferred_element_type=F32)
            run = run + jnp.sum(lk, axis=1, keepdims=True)
            out.append((run, acc))
        return tuple(out)

    zero = (jnp.zeros((tq, 1), F32), jnp.zeros((tq, LANES), F32))
    top = (qi * tq) // kb
    carry = block(top, (zero, zero), True)

    def body(i, carry):
        return block(top - 1 - i, carry, False)

    carry = lax.fori_loop(0, top, body, carry)
    o_ref[...] = carry[0][1] + carry[1][1]


def sb_prompt(yin, bias, nb, seq):
    assert seq % SB_KBLK == 0 and seq % SB_QBLK == 0
    nq = seq // SB_QBLK
    return pl.pallas_call(
        _sb_prompt_kernel,
        grid=(nb, SB_HEADS // 2, nq),
        in_specs=[pl.BlockSpec(memory_space=pltpu.SMEM),
                  pl.BlockSpec((SB_QBLK, LANES), lambda b, h, i: (b * nq + i, Q0 // LANES + h)),
                  pl.BlockSpec((seq, LANES), lambda b, h, i: (b, K0 // LANES + h)),
                  pl.BlockSpec((seq, LANES), lambda b, h, i: (b, V0 // LANES + h))],
        out_specs=pl.BlockSpec((SB_QBLK, LANES), lambda b, h, i: (b * nq + i, h)),
        out_shape=jax.ShapeDtypeStruct((nb * seq, SB_HEADS * SB_HEAD_DIM), F32),
        compiler_params=_cparams(("parallel", "parallel", "arbitrary")),
        name="sb_prompt",
    )(bias, yin, yin, yin)


SB_PAGES_PER_STEP = 8


def _sb_sample_kernel(pt_ref, bias_ref, q_ref, kn_ref, vn_ref, *refs, npp, dq):
    krefs = refs[:npp]
    vrefs = refs[npp:2 * npp]
    o_ref = refs[2 * npp]
    qbt_ref, run_ref, acc_ref = refs[2 * npp + 1:]
    j = pl.program_id(1)
    width = SB_HEADS * SB_HEAD_DIM
    hq = SB_HEADS * dq
    page = CHUNK
    rowh = lax.broadcasted_iota(jnp.int32, (hq, 1), 0) // dq
    colh = lax.broadcasted_iota(jnp.int32, (1, width), 1) // SB_HEAD_DIM
    diag = rowh == colh
    upper = (lax.broadcasted_iota(jnp.int32, (page, page), 0) >
             lax.broadcasted_iota(jnp.int32, (page, page), 1)).astype(BF16)

    def later_keys(lk):
        hi, lo = _split_bf16(lk)
        return jnp.dot(hi, upper, preferred_element_type=F32) + jnp.dot(lo, upper, preferred_element_type=F32)

    @pl.when(j == 0)
    def _():
        qs = q_ref[...] * (SB_HEAD_DIM ** -0.5)
        qrep = jnp.concatenate([qs] * SB_HEADS, axis=0)
        qb = jnp.where(diag, qrep, 0.0).astype(BF16)
        qbt_ref[...] = qb
        pad = jnp.zeros((page - dq, width), F32)
        kn = jnp.concatenate([kn_ref[...], pad], axis=0).astype(BF16)
        vn = jnp.concatenate([vn_ref[...], pad], axis=0).astype(BF16)
        z = lax.dot_general(qb, kn, (((1,), (1,)), ((), ())), preferred_element_type=F32) + bias_ref[...]
        vis = (lax.broadcasted_iota(jnp.int32, (1, page), 1) <
               lax.broadcasted_iota(jnp.int32, (hq, 1), 0) % dq)
        ls = _log_sigmoid(z)
        lk = jnp.where(vis, ls - z, 0.0)
        w = jnp.where(vis, jnp.exp(ls + later_keys(lk)), 0.0)
        acc_ref[...] = jnp.dot(w.astype(BF16), vn, preferred_element_type=F32)
        run_ref[...] = jnp.sum(lk, axis=1, keepdims=True)

    qb = qbt_ref[...]
    ls_all, later_all, sums = [], [], []
    for i in range(npp):
        z = jnp.dot(qb, krefs[i][...].astype(BF16), preferred_element_type=F32) + bias_ref[...]
        ls = _log_sigmoid(z)
        lk = ls - z
        ls_all.append(ls)
        later_all.append(later_keys(lk))
        sums.append(jnp.sum(lk, axis=1, keepdims=True))
    run = run_ref[...]
    ws = []
    for i in range(npp):
        ws.append(jnp.exp(ls_all[i] + later_all[i] + run).astype(BF16))
        run = run + sums[i]
    run_ref[...] = run
    wcat = jnp.concatenate(ws, axis=1)
    vcat = jnp.concatenate([vrefs[i][...].astype(BF16) for i in range(npp)], axis=1)
    acc_ref[...] += lax.dot_general(wcat, vcat, (((1,), (1,)), ((), ())), preferred_element_type=F32)

    @pl.when(j == pl.num_programs(1) - 1)
    def _():
        sel = jnp.where(diag, acc_ref[...], 0.0)
        o = sel[0:dq]
        for h in range(1, SB_HEADS):
            o = o + sel[h * dq:(h + 1) * dq]
        o_ref[...] = o


def sb_sample(yin, bias, cache_k, cache_v, page_table, nb, dq):
    n_pool, page = cache_k.shape[0], cache_k.shape[1]
    assert page == CHUNK and dq == 8
    width = SB_HEADS * SB_HEAD_DIM
    npages = page_table.shape[1]
    npp = SB_PAGES_PER_STEP
    assert npages % npp == 0
    hq = SB_HEADS * dq
    ck = jnp.transpose(cache_k, (0, 2, 3, 1)).reshape(n_pool, width, page)
    cv = jnp.transpose(cache_v, (0, 2, 3, 1)).reshape(n_pool, width, page)
    bias_l = jnp.repeat(bias, dq).reshape(hq, 1)

    def page_spec(i):
        return pl.BlockSpec((None, width, page),
                            lambda b, j, pt: (pt[b, npages - 1 - (j * npp + i)], 0, 0))

    grid_spec = pltpu.PrefetchScalarGridSpec(
        num_scalar_prefetch=1,
        grid=(nb, npages // npp),
        in_specs=[pl.BlockSpec((hq, 1), lambda b, j, pt: (0, 0)),
                  pl.BlockSpec((dq, width), lambda b, j, pt: (b, Q0 // width)),
                  pl.BlockSpec((dq, width), lambda b, j, pt: (b, K0 // width)),
                  pl.BlockSpec((dq, width), lambda b, j, pt: (b, V0 // width))]
                 + [page_spec(i) for i in range(npp)] + [page_spec(i) for i in range(npp)],
        out_specs=pl.BlockSpec((dq, width), lambda b, j, pt: (b, 0)),
        scratch_shapes=[pltpu.VMEM((hq, width), BF16),
                        pltpu.VMEM((hq, 1), F32),
                        pltpu.VMEM((hq, width), F32)],
    )
    return pl.pallas_call(
        functools.partial(_sb_sample_kernel, npp=npp, dq=dq),
        grid_spec=grid_spec,
        out_shape=jax.ShapeDtypeStruct((nb * dq, width), F32),
        compiler_params=_cparams(("parallel", "arbitrary")),
        name="sb_sample",
    )(page_table, bias_l, yin, yin, yin, *([ck] * npp), *([cv] * npp))


def _mem_attend_kernel(q_ref, k_ref, v_ref, w_ref, o_ref):
    for h in range(MEM_HEADS):
        sl = slice(h * MEM_HEAD_DIM, (h + 1) * MEM_HEAD_DIM)
        q = q_ref[:, sl]
        ms = jnp.mean(q * q, axis=-1, keepdims=True)
        qn = (q * lax.rsqrt(ms + RMS_EPS) * w_ref[...]).astype(BF16)
        s = lax.dot_general(qn, k_ref[:, sl].astype(BF16), (((1,), (1,)), ((), ())),
                            preferred_element_type=F32) * (MEM_HEAD_DIM ** -0.5)
        e = jnp.exp(s - jnp.max(s, axis=-1, keepdims=True))
        p = e / jnp.sum(e, axis=-1, keepdims=True)
        o_ref[:, sl] = jnp.dot(p.astype(BF16), v_ref[:, sl].astype(BF16), preferred_element_type=F32)


def mem_attend(yin, mem_k, mem_v, qnw, nb, seq, tq):
    width = MEM_HEADS * MEM_HEAD_DIM
    mt = mem_k.shape[1]
    nq = seq // tq
    return pl.pallas_call(
        _mem_attend_kernel,
        grid=(nb, nq),
        in_specs=[pl.BlockSpec((tq, width), lambda b, i: (b * nq + i, QM0 // width)),
                  pl.BlockSpec((None, mt, width), lambda b, i: (b, 0, 0)),
                  pl.BlockSpec((None, mt, width), lambda b, i: (b, 0, 0)),
                  pl.BlockSpec((1, MEM_HEAD_DIM), lambda b, i: (0, 0))],
        out_specs=pl.BlockSpec((tq, width), lambda b, i: (b * nq + i, 0)),
        out_shape=jax.ShapeDtypeStruct((nb * seq, width), F32),
        compiler_params=_cparams(("parallel", "arbitrary")),
        name="mem_attend",
    )(yin, mem_k, mem_v, qnw.reshape(1, MEM_HEAD_DIM))


def _merge_kernel(x_ref, y_ref, sb_ref, mm_ref, g1_ref, g2_ref, g3_ref, w1_ref, w2_ref, w3_ref, wo_ref, o_ref):
    b1 = jnp.dot(y_ref[...].astype(BF16), w1_ref[...], preferred_element_type=F32)
    b2 = jnp.dot(sb_ref[...].astype(BF16), w2_ref[...], preferred_element_type=F32)
    b3 = jnp.dot(mm_ref[...].astype(BF16), w3_ref[...], preferred_element_type=F32)
    merged = (jax.nn.sigmoid(g1_ref[...]) * b1 + jax.nn.sigmoid(g2_ref[...]) * b2 +
              jax.nn.sigmoid(g3_ref[...]) * b3)
    o_ref[...] = x_ref[...] + jnp.dot(merged.astype(BF16), wo_ref[...], preferred_element_type=F32)


def merge(x, yin, y_ssm, o_sb, o_mem, w1, w2, w3, wo, tm):
    t, d = x.shape

    def tok(width, col=0):
        return pl.BlockSpec((tm, width), lambda i: (i, col))

    def full(w):
        return pl.BlockSpec(w.shape, lambda i: (0, 0))

    return pl.pallas_call(
        _merge_kernel,
        grid=(t // tm,),
        in_specs=[tok(d), tok(D_INNER), tok(d), tok(d),
                  tok(d, G0 // d), tok(d, G0 // d + 1), tok(d, G0 // d + 2),
                  full(w1), full(w2), full(w3), full(wo)],
        out_specs=tok(d),
        out_shape=jax.ShapeDtypeStruct((t, d), F32),
        compiler_params=_cparams(("parallel",)),
        name="merge",
    )(x, y_ssm, o_sb, o_mem, yin, yin, yin, w1, w2, w3, wo)


PEER_TB = 256


def _top_values(s, n):
    out = []
    for _ in range(n):
        m = jnp.max(s, axis=0, keepdims=True)
        out.append(m)
        s = jnp.where(s == m, NEG, s)
    return out


def _peer_stats_kernel(q_ref, keys_ref, s0_ref, s1_ref, st_ref):
    tb = q_ref.shape[0]
    taus, a0s, b0s, izs = [], [], [], []
    for h in range(PEER_HEADS):
        tops = []
        for c in range(2):
            col = (h * 2 + c) * N_KEYS
            s = lax.dot_general(keys_ref[c], q_ref[:, col:col + N_KEYS].astype(BF16),
                                (((1,), (1,)), ((), ())), preferred_element_type=F32)
            (s0_ref if c == 0 else s1_ref)[h] = s
            tops.append(_top_values(s, PEER_TOPK))
        b = jnp.concatenate(tops[1], axis=0)
        cand = jnp.concatenate([a + b for a in tops[0]], axis=0)
        best = _top_values(cand, PEER_TOPK)
        zsum = jnp.zeros((1, tb), F32)
        for m in best:
            zsum = zsum + jnp.exp(m - best[0])
        taus.append(best[-1])
        a0s.append(tops[0][0])
        b0s.append(tops[1][0])
        izs.append(1.0 / zsum)
    st_ref[0] = jnp.concatenate(taus, axis=0)
    st_ref[1] = jnp.concatenate(a0s, axis=0)
    st_ref[2] = jnp.concatenate(b0s, axis=0)
    st_ref[3] = jnp.concatenate(izs, axis=0)


def peer_stats(qp, keys):
    t = qp.shape[0]
    tb = PEER_TB
    return pl.pallas_call(
        _peer_stats_kernel,
        grid=(t // tb,),
        in_specs=[pl.BlockSpec((tb, qp.shape[1]), lambda i: (i, 0)),
                  pl.BlockSpec(keys.shape, lambda i: (0, 0, 0))],
        out_specs=[pl.BlockSpec((PEER_HEADS, N_KEYS, tb), lambda i: (0, 0, i)),
                   pl.BlockSpec((PEER_HEADS, N_KEYS, tb), lambda i: (0, 0, i)),
                   pl.BlockSpec((4, PEER_HEADS, tb), lambda i: (0, 0, i))],
        out_shape=[jax.ShapeDtypeStruct((PEER_HEADS, N_KEYS, t), F32),
                   jax.ShapeDtypeStruct((PEER_HEADS, N_KEYS, t), F32),
                   jax.ShapeDtypeStruct((4, PEER_HEADS, t), F32)],
        compiler_params=_cparams(("parallel",)),
        name="peer_stats",
    )(qp, keys)


PEER_TBK = 512
PEER_EB = 512


def _peer_expert_kernel(x_ref, nw_ref, wu_ref, wd_ref, s0_ref, s1_ref, st_ref, o_ref,
                        xt_ref, acc_ref, p0_ref, p1_ref, act_ref, coef_ref):
    e = pl.program_id(1)
    ne = pl.num_programs(1)
    nsub = PEER_EB // N_KEYS

    @pl.when(e == 0)
    def _():
        x = x_ref[...]
        ms = jnp.mean(x * x, axis=-1, keepdims=True)
        xt_ref[...] = (x * lax.rsqrt(ms + RMS_EPS) * nw_ref[...]).T.astype(BF16)
        acc_ref[...] = jnp.zeros_like(acc_ref)
        for h in range(PEER_HEADS):
            p0_ref[h] = jnp.exp(s0_ref[h] - st_ref[1, h:h + 1, :])
            p1_ref[h] = jnp.exp(s1_ref[h] - st_ref[2, h:h + 1, :]) * st_ref[3, h:h + 1, :]

    act_ref[...] = jnp.dot(wu_ref[...], xt_ref[...], preferred_element_type=F32)
    for ii in range(nsub):
        i = e * nsub + ii
        rs = slice(ii * N_KEYS, (ii + 1) * N_KEYS)
        s0_rows = [s0_ref[h, pl.ds(i, 1), :] for h in range(PEER_HEADS)]
        p0_rows = [p0_ref[h, pl.ds(i, 1), :] for h in range(PEER_HEADS)]
        for tc in range(x_ref.shape[0] // LANES):
            cs = slice(tc * LANES, (tc + 1) * LANES)
            gate = jnp.zeros((N_KEYS, LANES), F32)
            for h in range(PEER_HEADS):
                score = s0_rows[h][:, cs] + s1_ref[h, :, cs]
                pr = p0_rows[h][:, cs] * p1_ref[h, :, cs]
                gate = gate + jnp.where(score >= st_ref[0, h:h + 1, cs], pr, 0.0)
            a = act_ref[rs, cs]
            gelu = 0.5 * a * (1.0 + lax.erf(a * (2.0 ** -0.5)))
            coef_ref[rs, cs] = (gate * gelu).astype(BF16)
    acc_ref[...] += jnp.dot(wd_ref[...], coef_ref[...], preferred_element_type=F32)

    @pl.when(e == ne - 1)
    def _():
        o_ref[...] = x_ref[...] + acc_ref[...].T


def peer_experts(x, nw, wu, wdt, s0t, s1t, st):
    t, d = x.shape
    tb, eb = PEER_TBK, PEER_EB
    ne = wu.shape[0] // eb
    return pl.pallas_call(
        _peer_expert_kernel,
        grid=(t // tb, ne),
        in_specs=[pl.BlockSpec((tb, d), lambda i, e: (i, 0)),
                  pl.BlockSpec((1, d), lambda i, e: (0, 0)),
                  pl.BlockSpec((eb, d), lambda i, e: (e, 0)),
                  pl.BlockSpec((d, eb), lambda i, e: (0, e)),
                  pl.BlockSpec((PEER_HEADS, N_KEYS, tb), lambda i, e: (0, 0, i)),
                  pl.BlockSpec((PEER_HEADS, N_KEYS, tb), lambda i, e: (0, 0, i)),
                  pl.BlockSpec((4, PEER_HEADS, tb), lambda i, e: (0, 0, i))],
        out_specs=pl.BlockSpec((tb, d), lambda i, e: (i, 0)),
        out_shape=jax.ShapeDtypeStruct((t, d), F32),
        scratch_shapes=[pltpu.VMEM((d, tb), BF16),
                        pltpu.VMEM((d, tb), F32),
                        pltpu.VMEM((PEER_HEADS, N_KEYS, tb), F32),
                        pltpu.VMEM((PEER_HEADS, N_KEYS, tb), F32),
                        pltpu.VMEM((eb, tb), F32),
                        pltpu.VMEM((eb, tb), BF16)],
        compiler_params=_cparams(("parallel", "arbitrary")),
        name="peer_experts",
    )(x, nw.reshape(1, d), wu, wdt, s0t, s1t, st)


def _pad_rows(x, mult):
    t = x.shape[0]
    pad = (-t) % mult
    return x if pad == 0 else jnp.pad(x, ((0, pad), (0, 0)))


def _pick(t, pref):
    return pref if t % pref == 0 else t


def _peer(x1, wts):
    t = x1.shape[0]
    xp = _pad_rows(x1, PEER_TBK)
    qp = norm_matmul(xp, wts["norm2_w"], wts["peer_w_q"], _pick(xp.shape[0], 1024), 512)
    s0t, s1t, st = peer_stats(qp, wts["peer_keys"])
    out = peer_experts(xp, wts["norm2_w"], wts["peer_w_up"], wts["peer_w_down_t"], s0t, s1t, st)
    return out[:t]


def _layer(x, mem_k, mem_v, conv_prev, ssm_prev, wts, sb_fn):
    nb, seq, d = x.shape
    t = nb * seq
    xf = x.reshape(t, d)
    yin = norm_matmul(xf, wts["norm1_w"], wts["w_in"], _pick(t, 1024), 512)
    y_ssm, ssm_new = ssd(yin, conv_prev, wts["conv_wg"], wts["ssd_hp"], wts["ssd_hx"], ssm_prev, nb, seq)
    o_sb = sb_fn(yin)
    o_mem = mem_attend(yin, mem_k, mem_v, wts["q_norm_w"], nb, seq, _pick(seq, 512))
    x1 = merge(xf, yin, y_ssm, o_sb, o_mem, wts["w_ssm_out"], wts["w_sb_out"], wts["w_mem_out"], wts["w_out"],
               _pick(t, 256))
    y = _peer(x1, wts)
    y3 = yin.reshape(nb, seq, N_IN)
    k_new = y3[:, :, K0:K0 + d].reshape(nb, seq, SB_HEADS, SB_HEAD_DIM)
    v_new = y3[:, :, V0:V0 + d].reshape(nb, seq, SB_HEADS, SB_HEAD_DIM)
    xbc_raw = y3[:, :, XS0:XS0 + CONV_DIM]
    return y.reshape(nb, seq, d), k_new, v_new, xbc_raw, ssm_new


def _group_cols(a):
    lead = a.shape[:-1]
    xs = a[..., :D_INNER].reshape(lead + (SSM_GROUPS, GROUP_W))
    bm = a[..., D_INNER:D_INNER + SSM_GROUPS * D_STATE].reshape(lead + (SSM_GROUPS, D_STATE))
    cm = a[..., D_INNER + SSM_GROUPS * D_STATE:].reshape(lead + (SSM_GROUPS, D_STATE))
    return jnp.concatenate([xs, bm, cm], axis=-1)


def _prev_rows(prev):
    g = jnp.moveaxis(_group_cols(prev), 2, 1)
    return jnp.pad(g, ((0, 0), (0, 0), (8 - (CONV_W - 1), 0), (0, 0)))


def _prep_weights(l, norm1_w, w_in, conv_w, conv_b, dt_bias, a_log, d_skip, ssm_norm_w, w_ssm_out, w_sb_out,
                  sb_bias, mem_norm_w, w_mem_kv, q_norm_w, k_norm_w, w_mem_out, w_out, norm2_w, peer_w_q,
                  peer_sub_keys, peer_w_up, peer_w_down):
    hpg = SSM_HEADS // SSM_GROUPS
    w = w_in[l]
    dt_cols = w[:, D_INNER + CONV_DIM:D_INNER + CONV_DIM + SSM_HEADS]
    rest = w[:, D_INNER + CONV_DIM + SSM_HEADS:]
    dtc = jnp.pad(dt_cols.reshape(D_MODEL, SSM_GROUPS, hpg), ((0, 0), (0, 0), (0, LANES - hpg)))
    w_fused = jnp.concatenate([w[:, :D_INNER + CONV_DIM], rest, dtc.reshape(D_MODEL, SSM_GROUPS * LANES),
                               jnp.repeat(dt_cols, SSM_HEAD_DIM, axis=1)], axis=1).astype(BF16)
    assert w_fused.shape[1] == N_IN

    def compact(v):
        return jnp.pad(v.reshape(SSM_GROUPS, hpg), ((0, 0), (0, LANES - hpg)))

    def expand(v):
        return jnp.repeat(v, SSM_HEAD_DIM).reshape(SSM_GROUPS, GROUP_W)

    zc = jnp.zeros((SSM_GROUPS, LANES), F32)
    zx = jnp.zeros((SSM_GROUPS, GROUP_W), F32)
    ssd_hp = jnp.stack([compact(dt_bias[l]), compact(a_log[l])] + [zc] * 6, axis=1)
    ssd_hx = jnp.stack([expand(dt_bias[l]), expand(a_log[l]), expand(d_skip[l]),
                        ssm_norm_w[l].reshape(SSM_GROUPS, GROUP_W)] + [zx] * 4, axis=1)
    conv_rows = jnp.concatenate([conv_w[l], conv_b[l][None]], axis=0)
    conv_wg = jnp.pad(jnp.moveaxis(_group_cols(conv_rows), 1, 0), ((0, 0), (0, 8 - (CONV_W + 1)), (0, 0)))
    return dict(
        norm1_w=norm1_w[l], w_in=w_fused, conv_wg=conv_wg, ssd_hp=ssd_hp, ssd_hx=ssd_hx,
        w_ssm_out=w_ssm_out[l].astype(BF16), w_sb_out=w_sb_out[l].astype(BF16), sb_bias=sb_bias[l],
        mem_norm_w=mem_norm_w[l], w_mem_kv=w_mem_kv[l].astype(BF16), q_norm_w=q_norm_w[l],
        k_norm_w=k_norm_w[l], w_mem_out=w_mem_out[l].astype(BF16), w_out=w_out[l].astype(BF16),
        norm2_w=norm2_w[l], peer_w_q=peer_w_q[l].astype(BF16), peer_keys=peer_sub_keys[l].astype(BF16),
        peer_w_up=peer_w_up[l].astype(BF16), peer_w_down_t=peer_w_down[l].T.astype(BF16))


def kernel(x_prompt, x_sample, cache_sb_k, cache_sb_v, cache_mem_k, cache_mem_v, state_conv, state_ssm, page_table, mem_prompt, norm1_w, w_in, conv_w, conv_b, dt_bias, a_log, d_skip, ssm_norm_w, w_ssm_out, w_sb_out, sb_bias, mem_norm_w, w_mem_kv, q_norm_w, k_norm_w, w_mem_out, w_out, norm2_w, peer_w_q, peer_sub_keys, peer_w_up, peer_w_down):
    bp, seq_p, d = x_prompt.shape
    bs, seq_s, _ = x_sample.shape
    depth = w_in.shape[0]
    mt = mem_prompt.shape[1]
    mem_w = MEM_HEADS * MEM_HEAD_DIM
    y_p, y_s = x_prompt, x_sample
    outs = [[] for _ in range(10)]
    for l in range(depth):
        wts = _prep_weights(l, norm1_w, w_in, conv_w, conv_b, dt_bias, a_log, d_skip, ssm_norm_w, w_ssm_out,
                            w_sb_out, sb_bias, mem_norm_w, w_mem_kv, q_norm_w, k_norm_w, w_mem_out, w_out,
                            norm2_w, peer_w_q, peer_sub_keys, peer_w_up, peer_w_down)
        kv = norm_matmul(mem_prompt.reshape(bp * mt, d), wts["mem_norm_w"], wts["w_mem_kv"],
                         _pick(bp * mt, 1024), 512)
        mk = headnorm(kv, wts["k_norm_w"], _pick(bp * mt, 512)).reshape(bp, mt, mem_w)
        mv = kv[:, mem_w:].reshape(bp, mt, mem_w)
        prev0 = jnp.zeros((bp, SSM_GROUPS, 8, GROUP_W + 2 * D_STATE), F32)
        y_p, kp, vp, xbc_p, sp = _layer(
            y_p, mk, mv, prev0, None, wts,
            lambda yin: sb_prompt(yin, wts["sb_bias"], bp, seq_p))
        y_s, ks, vs, xbc_s, ss = _layer(
            y_s, cache_mem_k[l].reshape(bs, mt, mem_w), cache_mem_v[l].reshape(bs, mt, mem_w),
            _prev_rows(state_conv[l]), state_ssm[l], wts,
            lambda yin: sb_sample(yin, wts["sb_bias"], cache_sb_k[l], cache_sb_v[l], page_table, bs, seq_s))
        conv_s = jnp.concatenate([state_conv[l], xbc_s], axis=1)[:, -(CONV_W - 1):]
        for lst, v in zip(outs, [kp, vp, mk.reshape(bp, mt, MEM_HEADS, MEM_HEAD_DIM),
                                 mv.reshape(bp, mt, MEM_HEADS, MEM_HEAD_DIM), xbc_p[:, -(CONV_W - 1):], sp,
                                 ks, vs, conv_s, ss]):
            lst.append(v)
    return (y_p, y_s) + tuple(jnp.stack(o) for o in outs)
```

```python
import functools

import jax
import jax.numpy as jnp
from jax import lax
from jax.experimental import pallas as pl
from jax.experimental.pallas import tpu as pltpu

F32 = jnp.float32
BF16 = jnp.bfloat16

D_MODEL = 1024
D_INNER = 2048
SSM_HEADS = 32
SSM_HEAD_DIM = 64
SSM_GROUPS = 4
GROUP_W = D_INNER // SSM_GROUPS
D_STATE = 128
CONV_W = 4
CONV_DIM = D_INNER + 2 * SSM_GROUPS * D_STATE
SB_HEADS = 16
SB_HEAD_DIM = 64
MEM_HEADS = 4
MEM_HEAD_DIM = 256
PEER_HEADS = 8
N_KEYS = 128
PEER_TOPK = 16
RMS_EPS = 1e-6
NEG = -1e30

LANES = 128
CHUNK = 128
VMEM_LIMIT = 56 * 1024 * 1024

Z0 = 0
XS0 = Z0 + D_INNER
B0 = XS0 + D_INNER
C0 = B0 + SSM_GROUPS * D_STATE
Q0 = C0 + SSM_GROUPS * D_STATE
K0 = Q0 + D_MODEL
V0 = K0 + D_MODEL
QM0 = V0 + D_MODEL
G0 = QM0 + D_MODEL
DTC0 = G0 + 3 * D_MODEL
DTX0 = DTC0 + SSM_GROUPS * LANES
N_IN = DTX0 + D_INNER


def _cparams(sem):
    return pltpu.CompilerParams(dimension_semantics=sem, vmem_limit_bytes=VMEM_LIMIT)


def _softplus(v):
    return jnp.maximum(v, 0.0) + jnp.log(1.0 + jnp.exp(-jnp.abs(v)))


def _log_sigmoid(v):
    return jnp.minimum(v, 0.0) - jnp.log(1.0 + jnp.exp(-jnp.abs(v)))


def _split_bf16(v):
    hi = v.astype(BF16)
    lo = (v - hi.astype(F32)).astype(BF16)
    return hi, lo


def _norm_matmul_kernel(x_ref, nw_ref, w_ref, o_ref, h_ref):
    @pl.when(pl.program_id(1) == 0)
    def _():
        x = x_ref[...]
        ms = jnp.mean(x * x, axis=-1, keepdims=True)
        h_ref[...] = (x * lax.rsqrt(ms + RMS_EPS) * nw_ref[...]).astype(BF16)

    o_ref[...] = jnp.dot(h_ref[...], w_ref[...], preferred_element_type=F32)


def norm_matmul(x, nw, w, tm, tn):
    t, d = x.shape
    n = w.shape[1]
    return pl.pallas_call(
        _norm_matmul_kernel,
        grid=(t // tm, n // tn),
        in_specs=[pl.BlockSpec((tm, d), lambda i, j: (i, 0)),
                  pl.BlockSpec((1, d), lambda i, j: (0, 0)),
                  pl.BlockSpec((d, tn), lambda i, j: (0, j))],
        out_specs=pl.BlockSpec((tm, tn), lambda i, j: (i, j)),
        out_shape=jax.ShapeDtypeStruct((t, n), F32),
        scratch_shapes=[pltpu.VMEM((tm, d), BF16)],
        compiler_params=_cparams(("parallel", "arbitrary")),
        name="norm_matmul",
    )(x, nw.reshape(1, d), w)


def _headnorm_kernel(x_ref, w_ref, o_ref):
    for h in range(MEM_HEADS):
        sl = slice(h * MEM_HEAD_DIM, (h + 1) * MEM_HEAD_DIM)
        x = x_ref[:, sl]
        ms = jnp.mean(x * x, axis=-1, keepdims=True)
        o_ref[:, sl] = x * lax.rsqrt(ms + RMS_EPS) * w_ref[...]


def headnorm(kv, w, tm):
    t = kv.shape[0]
    width = MEM_HEADS * MEM_HEAD_DIM
    return pl.pallas_call(
        _headnorm_kernel,
        grid=(t // tm,),
        in_specs=[pl.BlockSpec((tm, width), lambda i: (i, 0)),
                  pl.BlockSpec((1, MEM_HEAD_DIM), lambda i: (0, 0))],
        out_specs=pl.BlockSpec((tm, width), lambda i: (i, 0)),
        out_shape=jax.ShapeDtypeStruct((t, width), F32),
        compiler_params=_cparams(("parallel",)),
        name="headnorm",
    )(kv, w.reshape(1, MEM_HEAD_DIM))


def _cumsum_rows(x):
    n = x.shape[0]
    r = lax.broadcasted_iota(jnp.int32, (n, 1), 0)
    s = 1
    while s < n:
        x = x + jnp.where(r >= s, pltpu.roll(x, s, axis=0), 0.0)
        s *= 2
    return x


def _ssd_kernel(*refs, qr, nc, has_init):
    if has_init:
        (xs_ref, b_ref, c_ref, z_ref, dtc_ref, dtx_ref, prev_ref, cw_ref, hp_ref, hx_ref, s0_ref,
         y_ref, so_ref, buf_ref, tail_ref, st_ref, dtb_ref) = refs
    else:
        (xs_ref, b_ref, c_ref, z_ref, dtc_ref, dtx_ref, prev_ref, cw_ref, hp_ref, hx_ref,
         y_ref, so_ref, buf_ref, tail_ref, st_ref, dtb_ref) = refs
        s0_ref = None
    q = CHUNK
    cw_w = GROUP_W + 2 * D_STATE
    c = pl.program_id(2)

    @pl.when(c == 0)
    def _():
        tail_ref[...] = prev_ref[...]
        if has_init:
            st_ref[...] = s0_ref[...].reshape(GROUP_W, D_STATE).T
        else:
            st_ref[...] = jnp.zeros_like(st_ref)

    buf_ref[0:8, :] = tail_ref[...]
    if qr < q:
        buf_ref[8:, :] = jnp.zeros((q, cw_w), F32)
        dtb_ref[...] = jnp.zeros_like(dtb_ref)
    buf_ref[8:8 + qr, 0:GROUP_W] = xs_ref[...]
    buf_ref[8:8 + qr, GROUP_W:GROUP_W + D_STATE] = b_ref[...]
    buf_ref[8:8 + qr, GROUP_W + D_STATE:cw_w] = c_ref[...]
    dtb_ref[0:qr, 0:LANES] = dtc_ref[...]
    dtb_ref[0:qr, LANES:] = dtx_ref[...]
    if nc > 1:
        tail_ref[...] = buf_ref[q:q + 8, :]

    acc = jnp.broadcast_to(cw_ref[CONV_W:CONV_W + 1, :], (q, cw_w))
    for w in range(CONV_W):
        acc = acc + cw_ref[w:w + 1, :] * buf_ref[5 + w:5 + w + q, :]
    xbc = acc * jax.nn.sigmoid(acc)
    xs = xbc[:, 0:GROUP_W]
    bm = xbc[:, GROUP_W:GROUP_W + D_STATE]
    cm = xbc[:, GROUP_W + D_STATE:cw_w]

    rows = lax.broadcasted_iota(jnp.int32, (q, 1), 0)
    dtc = _softplus(dtb_ref[:, 0:LANES] + hp_ref[0:1, :])
    dtx = _softplus(dtb_ref[:, LANES:] + hx_ref[0:1, :])
    if qr < q:
        dtc = jnp.where(rows < qr, dtc, 0.0)
        dtx = jnp.where(rows < qr, dtx, 0.0)
    csc = _cumsum_rows(dtc * (-jnp.exp(hp_ref[1:2, :])))
    csx = _cumsum_rows(dtx * (-jnp.exp(hx_ref[1:2, :])))
    cst = csc.T
    cb = lax.dot_general(cm.astype(BF16), bm.astype(BF16), (((1,), (1,)), ((), ())),
                         preferred_element_type=F32)
    tri = rows >= lax.broadcasted_iota(jnp.int32, (1, q), 1)
    tot = csx[q - 1:q, :]
    xdt = xs * dtx
    wgt = xdt * jnp.exp(tot - csx)
    ecs = jnp.exp(csc)
    st = st_ref[...]
    half = lax.broadcasted_iota(jnp.int32, (1, LANES), 1) // SSM_HEAD_DIM
    ys = []
    for pr in range(GROUP_W // LANES):
        xp = xdt[:, pr * LANES:(pr + 1) * LANES]
        sp = st[:, pr * LANES:(pr + 1) * LANES]
        ya = jnp.zeros((q, LANES), F32)
        for hh in range(LANES // SSM_HEAD_DIM):
            r = pr * (LANES // SSM_HEAD_DIM) + hh
            seg = csc[:, r:r + 1] - cst[r:r + 1, :]
            m = (cb * jnp.exp(jnp.where(tri, seg, NEG))).astype(BF16)
            ce = (cm * ecs[:, r:r + 1]).astype(BF16)
            xm = jnp.where(half == hh, xp, 0.0).astype(BF16)
            sm = jnp.where(half == hh, sp, 0.0).astype(BF16)
            ya = ya + jnp.dot(m, xm, preferred_element_type=F32)
            ya = ya + jnp.dot(ce, sm, preferred_element_type=F32)
        ys.append(ya)
    y = jnp.concatenate(ys, axis=1) + hx_ref[2:3, :] * xs

    new_st = st * jnp.exp(tot) + jnp.dot(bm.T.astype(BF16), wgt.astype(BF16), preferred_element_type=F32)
    st_ref[...] = new_st

    @pl.when(c == nc - 1)
    def _():
        so_ref[...] = new_st.T.reshape(GROUP_W // SSM_HEAD_DIM, SSM_HEAD_DIM, D_STATE)

    z = z_ref[...]
    g = y[0:qr] * (z * jax.nn.sigmoid(z))
    ms = jnp.mean(g * g, axis=-1, keepdims=True)
    y_ref[...] = g * lax.rsqrt(ms + RMS_EPS) * hx_ref[3:4, :]


def ssd(yin, prev, cw, hp, hx, s0, nb, seq):
    if seq % CHUNK == 0:
        qr, nc = CHUNK, seq // CHUNK
    else:
        assert seq < CHUNK and seq % 8 == 0
        qr, nc = seq, 1
    has_init = s0 is not None
    cw_w = GROUP_W + 2 * D_STATE
    hpg = SSM_HEADS // SSM_GROUPS

    def row(b, g, c):
        return b * nc + c

    in_specs = [
        pl.BlockSpec((qr, GROUP_W), lambda b, g, c: (row(b, g, c), XS0 // GROUP_W + g)),
        pl.BlockSpec((qr, D_STATE), lambda b, g, c: (row(b, g, c), B0 // D_STATE + g)),
        pl.BlockSpec((qr, D_STATE), lambda b, g, c: (row(b, g, c), C0 // D_STATE + g)),
        pl.BlockSpec((qr, GROUP_W), lambda b, g, c: (row(b, g, c), Z0 // GROUP_W + g)),
        pl.BlockSpec((qr, LANES), lambda b, g, c: (row(b, g, c), DTC0 // LANES + g)),
        pl.BlockSpec((qr, GROUP_W), lambda b, g, c: (row(b, g, c), DTX0 // GROUP_W + g)),
        pl.BlockSpec((None, None, 8, cw_w), lambda b, g, c: (b, g, 0, 0)),
        pl.BlockSpec((None, 8, cw_w), lambda b, g, c: (g, 0, 0)),
        pl.BlockSpec((None, 8, LANES), lambda b, g, c: (g, 0, 0)),
        pl.BlockSpec((None, 8, GROUP_W), lambda b, g, c: (g, 0, 0)),
    ]
    args = [yin, yin, yin, yin, yin, yin, prev, cw, hp, hx]
    if has_init:
        in_specs.append(pl.BlockSpec((None, hpg, SSM_HEAD_DIM, D_STATE), lambda b, g, c: (b, g, 0, 0)))
        args.append(s0)
    return pl.pallas_call(
        functools.partial(_ssd_kernel, qr=qr, nc=nc, has_init=has_init),
        grid=(nb, SSM_GROUPS, nc),
        in_specs=in_specs,
        out_specs=[pl.BlockSpec((qr, GROUP_W), lambda b, g, c: (row(b, g, c), g)),
                   pl.BlockSpec((None, hpg, SSM_HEAD_DIM, D_STATE), lambda b, g, c: (b, g, 0, 0))],
        out_shape=[jax.ShapeDtypeStruct((nb * seq, D_INNER), F32),
                   jax.ShapeDtypeStruct((nb, SSM_HEADS, SSM_HEAD_DIM, D_STATE), F32)],
        scratch_shapes=[pltpu.VMEM((CHUNK + 8, cw_w), F32),
                        pltpu.VMEM((8, cw_w), F32),
                        pltpu.VMEM((D_STATE, GROUP_W), F32),
                        pltpu.VMEM((CHUNK, LANES + GROUP_W), F32)],
        compiler_params=_cparams(("parallel", "parallel", "arbitrary")),
        name="ssd",
    )(*args)


SB_KBLK = 256
SB_QBLK = 256


def _sb_prompt_kernel(bias_ref, q_ref, k_ref, v_ref, o_ref):
    hp = pl.program_id(1)
    qi = pl.program_id(2)
    tq = SB_QBLK
    kb = SB_KBLK
    scale = SB_HEAD_DIM ** -0.5
    lane = lax.broadcasted_iota(jnp.int32, (1, LANES), 1)
    q = q_ref[...] * scale
    qstack = jnp.concatenate([jnp.where(lane // SB_HEAD_DIM == hh, q, 0.0) for hh in range(2)],
                             axis=0).astype(BF16)
    bias = [bias_ref[2 * hp + hh] for hh in range(2)]
    upper = (lax.broadcasted_iota(jnp.int32, (kb, kb), 0) >
             lax.broadcasted_iota(jnp.int32, (kb, kb), 1)).astype(BF16)
    qpos = qi * tq + lax.broadcasted_iota(jnp.int32, (tq, 1), 0)
    kcol = lax.broadcasted_iota(jnp.int32, (1, kb), 1)

    def block(kblk, carry, masked):
        runs, acc = carry
        start = pl.multiple_of(kblk * kb, kb)
        k = k_ref[pl.ds(start, kb), :].astype(BF16)
        v = v_ref[pl.ds(start, kb), :]
        vis = (start + kcol) < qpos
        zz = lax.dot_general(qstack, k, (((1,), (1,)), ((), ())), preferred_element_type=F32)
        lss, lks = [], []
        for hh in range(2):
            z = zz[hh * tq:(hh + 1) * tq] + bias[hh]
            ls = _log_sigmoid(z)
            lk = ls - z
            if masked:
                lk = jnp.where(vis, lk, 0.0)
            lss.append(ls)
            lks.append(lk)
        parts = [_split_bf16(lk) for lk in lks]
        stack = jnp.concatenate([p[0] for p in parts] + [p[1] for p in parts], axis=0)
        later = jnp.dot(stack, upper, preferred_element_type=F32)
        ws, vms, new_runs = [], [], []
        for hh in range(2):
            w = jnp.exp(lss[hh] + later[hh * tq:(hh + 1) * tq] + later[(2 + hh) * tq:(3 + hh) * tq] + runs[hh])
            if masked:
                w = jnp.where(vis, w, 0.0)
            ws.append(w.astype(BF16))
            vms.append(jnp.where(lane // SB_HEAD_DIM == hh, v, 0.0).astype(BF16))
            new_runs.append(runs[hh] + jnp.sum(lks[hh], axis=1, keepdims=True))
        acc = acc + jnp.dot(jnp.concatenate(ws, axis=1), jnp.concatenate(vms, axis=0),
                            preferred_element_type=F32)
        return tuple(new_runs), acc

    zero = jnp.zeros((tq, 1), F32)
    top = (qi * tq) // kb
    carry = block(top, ((zero, zero), jnp.zeros((tq, LANES), F32)), True)

    def body(i, carry):
        return block(top - 1 - i, carry, False)

    carry = lax.fori_loop(0, top, body, carry)
    o_ref[...] = carry[1]


def sb_prompt(yin, bias, nb, seq):
    assert seq % SB_KBLK == 0 and seq % SB_QBLK == 0
    nq = seq // SB_QBLK
    return pl.pallas_call(
        _sb_prompt_kernel,
        grid=(nb, SB_HEADS // 2, nq),
        in_specs=[pl.BlockSpec(memory_space=pltpu.SMEM),
                  pl.BlockSpec((SB_QBLK, LANES), lambda b, h, i: (b * nq + i, Q0 // LANES + h)),
                  pl.BlockSpec((seq, LANES), lambda b, h, i: (b, K0 // LANES + h)),
                  pl.BlockSpec((seq, LANES), lambda b, h, i: (b, V0 // LANES + h))],
        out_specs=pl.BlockSpec((SB_QBLK, LANES), lambda b, h, i: (b * nq + i, h)),
        out_shape=jax.ShapeDtypeStruct((nb * seq, SB_HEADS * SB_HEAD_DIM), F32),
        compiler_params=_cparams(("parallel", "parallel", "arbitrary")),
        name="sb_prompt",
    )(bias, yin, yin, yin)


SB_PAGES_PER_STEP = 8


def _sb_sample_kernel(pt_ref, bias_ref, q_ref, kn_ref, vn_ref, *refs, npp, dq):
    krefs = refs[:npp]
    vrefs = refs[npp:2 * npp]
    o_ref = refs[2 * npp]
    qbt_ref, run_ref, acc_ref = refs[2 * npp + 1:]
    j = pl.program_id(1)
    width = SB_HEADS * SB_HEAD_DIM
    hq = SB_HEADS * dq
    page = CHUNK
    blk = 2 * page
    nblk = npp // 2
    rowh = lax.broadcasted_iota(jnp.int32, (hq, 1), 0) // dq
    colh = lax.broadcasted_iota(jnp.int32, (1, width), 1) // SB_HEAD_DIM
    diag = rowh == colh
    upper = (lax.broadcasted_iota(jnp.int32, (blk, blk), 0) >
             lax.broadcasted_iota(jnp.int32, (blk, blk), 1)).astype(BF16)

    @pl.when(j == 0)
    def _():
        qs = q_ref[...] * (SB_HEAD_DIM ** -0.5)
        qrep = jnp.concatenate([qs] * SB_HEADS, axis=0)
        qb = jnp.where(diag, qrep, 0.0).astype(BF16)
        qbt_ref[...] = qb
        pad = jnp.zeros((page - dq, width), F32)
        kn = jnp.concatenate([kn_ref[...], pad], axis=0).astype(BF16)
        vnt = jnp.concatenate([vn_ref[...], pad], axis=0).T.astype(BF16)
        z = lax.dot_general(qb, kn, (((1,), (1,)), ((), ())), preferred_element_type=F32) + bias_ref[...]
        vis = (lax.broadcasted_iota(jnp.int32, (1, page), 1) <
               lax.broadcasted_iota(jnp.int32, (hq, 1), 0) % dq)
        ls = _log_sigmoid(z)
        lk = jnp.where(vis, ls - z, 0.0)
        hi, lo = _split_bf16(lk)
        later = (jnp.dot(hi, upper[:page, :page], preferred_element_type=F32) +
                 jnp.dot(lo, upper[:page, :page], preferred_element_type=F32))
        w = jnp.where(vis, jnp.exp(ls + later), 0.0)
        acc_ref[...] = lax.dot_general(vnt, w.astype(BF16), (((1,), (1,)), ((), ())),
                                       preferred_element_type=F32)
        run_ref[...] = jnp.sum(lk, axis=1, keepdims=True)

    kcat = jnp.concatenate([krefs[i][...].astype(BF16) for i in range(npp)], axis=1)
    z = jnp.dot(qbt_ref[...], kcat, preferred_element_type=F32) + bias_ref[...]
    ls = _log_sigmoid(z)
    lk = ls - z
    hi, lo = _split_bf16(lk)
    stack = jnp.concatenate([hi[:, b * blk:(b + 1) * blk] for b in range(nblk)] +
                            [lo[:, b * blk:(b + 1) * blk] for b in range(nblk)], axis=0)
    later = jnp.dot(stack, upper, preferred_element_type=F32)
    run = run_ref[...]
    ws = [None] * nblk
    for b in reversed(range(nblk)):
        cs = slice(b * blk, (b + 1) * blk)
        later_b = later[b * hq:(b + 1) * hq] + later[(nblk + b) * hq:(nblk + b + 1) * hq]
        ws[b] = jnp.exp(ls[:, cs] + later_b + run).astype(BF16)
        run = run + jnp.sum(lk[:, cs], axis=1, keepdims=True)
    run_ref[...] = run
    wcat = jnp.concatenate(ws, axis=1)
    vcat = jnp.concatenate([vrefs[i][...].astype(BF16) for i in range(npp)], axis=1)
    acc_ref[...] += lax.dot_general(vcat, wcat, (((1,), (1,)), ((), ())), preferred_element_type=F32)

    @pl.when(j == pl.num_programs(1) - 1)
    def _():
        sel = jnp.where(diag, acc_ref[...].T, 0.0)
        o = sel[0:dq]
        for h in range(1, SB_HEADS):
            o = o + sel[h * dq:(h + 1) * dq]
        o_ref[...] = o


def sb_sample(yin, bias, cache_k, cache_v, page_table, nb, dq):
    n_pool, page = cache_k.shape[0], cache_k.shape[1]
    assert page == CHUNK and dq == 8
    width = SB_HEADS * SB_HEAD_DIM
    npages = page_table.shape[1]
    npp = SB_PAGES_PER_STEP
    assert npages % npp == 0
    hq = SB_HEADS * dq
    ck = jnp.transpose(cache_k, (0, 2, 3, 1)).reshape(n_pool, width, page)
    cv = jnp.transpose(cache_v, (0, 2, 3, 1)).reshape(n_pool, width, page)
    bias_l = jnp.repeat(bias, dq).reshape(hq, 1)

    def page_spec(i):
        return pl.BlockSpec((None, width, page),
                            lambda b, j, pt: (pt[b, npages - (j + 1) * npp + i], 0, 0))

    grid_spec = pltpu.PrefetchScalarGridSpec(
        num_scalar_prefetch=1,
        grid=(nb, npages // npp),
        in_specs=[pl.BlockSpec((hq, 1), lambda b, j, pt: (0, 0)),
                  pl.BlockSpec((dq, width), lambda b, j, pt: (b, Q0 // width)),
                  pl.BlockSpec((dq, width), lambda b, j, pt: (b, K0 // width)),
                  pl.BlockSpec((dq, width), lambda b, j, pt: (b, V0 // width))]
                 + [page_spec(i) for i in range(npp)] + [page_spec(i) for i in range(npp)],
        out_specs=pl.BlockSpec((dq, width), lambda b, j, pt: (b, 0)),
        scratch_shapes=[pltpu.VMEM((hq, width), BF16),
                        pltpu.VMEM((hq, 1), F32),
                        pltpu.VMEM((width, hq), F32)],
    )
    return pl.pallas_call(
        functools.partial(_sb_sample_kernel, npp=npp, dq=dq),
        grid_spec=grid_spec,
        out_shape=jax.ShapeDtypeStruct((nb * dq, width), F32),
        compiler_params=_cparams(("parallel", "arbitrary")),
        name="sb_sample",
    )(page_table, bias_l, yin, yin, yin, *([ck] * npp), *([cv] * npp))


MEM_DCHUNKS = MEM_HEAD_DIM // LANES


def _mem_attend_kernel(q_ref, k_ref, v_ref, w_ref, o_ref, *, tiled):
    def head(ref, h):
        if not tiled:
            return ref[:, h * MEM_HEAD_DIM:(h + 1) * MEM_HEAD_DIM]
        rows = ref.shape[0] // (MEM_DCHUNKS * MEM_HEADS)
        return jnp.concatenate(
            [ref[pl.ds(dc * MEM_HEADS + h, rows, stride=MEM_DCHUNKS * MEM_HEADS), :] for dc in range(MEM_DCHUNKS)],
            axis=1)

    for h in range(MEM_HEADS):
        sl = slice(h * MEM_HEAD_DIM, (h + 1) * MEM_HEAD_DIM)
        q = q_ref[:, sl]
        ms = jnp.mean(q * q, axis=-1, keepdims=True)
        qn = (q * lax.rsqrt(ms + RMS_EPS) * w_ref[...]).astype(BF16)
        s = lax.dot_general(qn, head(k_ref, h).astype(BF16), (((1,), (1,)), ((), ())),
                            preferred_element_type=F32) * (MEM_HEAD_DIM ** -0.5)
        e = jnp.exp(s - jnp.max(s, axis=-1, keepdims=True))
        p = e / jnp.sum(e, axis=-1, keepdims=True)
        o_ref[:, sl] = jnp.dot(p.astype(BF16), head(v_ref, h).astype(BF16), preferred_element_type=F32)


def _mem_tiled_view(cache):
    nb, mt = cache.shape[0], cache.shape[1]
    v = cache.reshape(nb, mt, MEM_HEADS, MEM_DCHUNKS, LANES)
    return jnp.transpose(v, (0, 1, 3, 2, 4)).reshape(nb, mt * MEM_DCHUNKS * MEM_HEADS, LANES)


def mem_attend(yin, mem_k, mem_v, qnw, nb, seq, tq, tiled=False):
    width = MEM_HEADS * MEM_HEAD_DIM
    nq = seq // tq
    kv_block = (None,) + tuple(mem_k.shape[1:])
    return pl.pallas_call(
        functools.partial(_mem_attend_kernel, tiled=tiled),
        grid=(nb, nq),
        in_specs=[pl.BlockSpec((tq, width), lambda b, i: (b * nq + i, QM0 // width)),
                  pl.BlockSpec(kv_block, lambda b, i: (b, 0, 0)),
                  pl.BlockSpec(kv_block, lambda b, i: (b, 0, 0)),
                  pl.BlockSpec((1, MEM_HEAD_DIM), lambda b, i: (0, 0))],
        out_specs=pl.BlockSpec((tq, width), lambda b, i: (b * nq + i, 0)),
        out_shape=jax.ShapeDtypeStruct((nb * seq, width), F32),
        compiler_params=_cparams(("parallel", "arbitrary")),
        name="mem_attend",
    )(yin, mem_k, mem_v, qnw.reshape(1, MEM_HEAD_DIM))


def _merge_kernel(x_ref, y_ref, sb_ref, mm_ref, g1_ref, g2_ref, g3_ref, w1_ref, w2_ref, w3_ref, wo_ref, o_ref):
    b1 = jnp.dot(y_ref[...].astype(BF16), w1_ref[...], preferred_element_type=F32)
    b2 = jnp.dot(sb_ref[...].astype(BF16), w2_ref[...], preferred_element_type=F32)
    b3 = jnp.dot(mm_ref[...].astype(BF16), w3_ref[...], preferred_element_type=F32)
    merged = (jax.nn.sigmoid(g1_ref[...]) * b1 + jax.nn.sigmoid(g2_ref[...]) * b2 +
              jax.nn.sigmoid(g3_ref[...]) * b3)
    o_ref[...] = x_ref[...] + jnp.dot(merged.astype(BF16), wo_ref[...], preferred_element_type=F32)


def merge(x, yin, y_ssm, o_sb, o_mem, w1, w2, w3, wo, tm):
    t, d = x.shape

    def tok(width, col=0):
        return pl.BlockSpec((tm, width), lambda i: (i, col))

    def full(w):
        return pl.BlockSpec(w.shape, lambda i: (0, 0))

    return pl.pallas_call(
        _merge_kernel,
        grid=(t // tm,),
        in_specs=[tok(d), tok(D_INNER), tok(d), tok(d),
                  tok(d, G0 // d), tok(d, G0 // d + 1), tok(d, G0 // d + 2),
                  full(w1), full(w2), full(w3), full(wo)],
        out_specs=tok(d),
        out_shape=jax.ShapeDtypeStruct((t, d), F32),
        compiler_params=_cparams(("parallel",)),
        name="merge",
    )(x, y_ssm, o_sb, o_mem, yin, yin, yin, w1, w2, w3, wo)


PEER_TB = 256


def _top_values(s, n):
    out = []
    for _ in range(n):
        m = jnp.max(s, axis=0, keepdims=True)
        out.append(m)
        s = jnp.where(s == m, NEG, s)
    return out


def _peer_stats_kernel(q_ref, keys_ref, th_ref, p0_ref, s1_ref, p1_ref):
    tb = q_ref.shape[0]
    big = -NEG
    for h in range(PEER_HEADS):
        tops, scores = [], []
        for c in range(2):
            col = (h * 2 + c) * N_KEYS
            s = lax.dot_general(keys_ref[c], q_ref[:, col:col + N_KEYS].astype(BF16),
                                (((1,), (1,)), ((), ())), preferred_element_type=F32)
            scores.append(s)
            tops.append(_top_values(s, PEER_TOPK))
        b = jnp.concatenate(tops[1], axis=0)
        row8 = lax.broadcasted_iota(jnp.int32, (8, 1), 0)
        pieces = [tops[0][0] + b]
        for a in range(1, 8):
            part = tops[0][a] + b[0:8]
            if PEER_TOPK // (a + 1) < 8:
                part = jnp.where(row8 < PEER_TOPK // (a + 1), part, NEG)
            pieces.append(part)
        pieces.append(jnp.concatenate(tops[0][8:], axis=0) + tops[1][0])
        cand = jnp.concatenate(pieces, axis=0)
        best = _top_values(cand, PEER_TOPK)
        zsum = jnp.zeros((1, tb), F32)
        for m in best:
            zsum = zsum + jnp.exp(m - best[0])
        tau = best[-1]
        thetas = [jnp.min(jnp.where(pieces[0] >= tau, b, big), axis=0, keepdims=True)]
        for a in range(1, 8):
            thetas.append(jnp.min(jnp.where(pieces[a] >= tau, b[0:8], big), axis=0, keepdims=True))
        tail = jnp.where(pieces[8] >= tau, tops[1][0], big)
        thetas += [tail[r:r + 1] for r in range(8)]
        theta = jnp.full((N_KEYS, tb), big, F32)
        for a in range(PEER_TOPK):
            theta = jnp.where(scores[0] == tops[0][a], thetas[a], theta)
        th_ref[h] = theta
        p0_ref[h] = jnp.exp(scores[0] - tops[0][0])
        s1_ref[h] = scores[1]
        p1_ref[h] = jnp.exp(scores[1] - tops[1][0]) * (1.0 / zsum)


def peer_stats(qp, keys):
    t = qp.shape[0]
    tb = PEER_TB
    return pl.pallas_call(
        _peer_stats_kernel,
        grid=(t // tb,),
        in_specs=[pl.BlockSpec((tb, qp.shape[1]), lambda i: (i, 0)),
                  pl.BlockSpec(keys.shape, lambda i: (0, 0, 0))],
        out_specs=[pl.BlockSpec((PEER_HEADS, N_KEYS, tb), lambda i: (0, 0, i))] * 4,
        out_shape=[jax.ShapeDtypeStruct((PEER_HEADS, N_KEYS, t), F32)] * 4,
        compiler_params=_cparams(("parallel",)),
        name="peer_stats",
    )(qp, keys)


PEER_TBK = 512
PEER_EB = 512


def _peer_expert_kernel(x_ref, nw_ref, wu_ref, wd_ref, th_ref, p0_ref, s1_ref, p1_ref, o_ref,
                        xt_ref, acc_ref, act_ref, coef_ref):
    e = pl.program_id(1)
    ne = pl.num_programs(1)
    nsub = PEER_EB // N_KEYS

    @pl.when(e == 0)
    def _():
        x = x_ref[...]
        ms = jnp.mean(x * x, axis=-1, keepdims=True)
        xt_ref[...] = (x * lax.rsqrt(ms + RMS_EPS) * nw_ref[...]).T.astype(BF16)
        acc_ref[...] = jnp.zeros_like(acc_ref)

    act_ref[...] = jnp.dot(wu_ref[...], xt_ref[...], preferred_element_type=F32)
    for ii in range(nsub):
        i = e * nsub + ii
        rs = slice(ii * N_KEYS, (ii + 1) * N_KEYS)
        th_rows = [th_ref[h, pl.ds(i, 1), :] for h in range(PEER_HEADS)]
        p0_rows = [p0_ref[h, pl.ds(i, 1), :] for h in range(PEER_HEADS)]
        for tc in range(x_ref.shape[0] // LANES):
            cs = slice(tc * LANES, (tc + 1) * LANES)
            gate = jnp.zeros((N_KEYS, LANES), F32)
            for h in range(PEER_HEADS):
                pr = p0_rows[h][:, cs] * p1_ref[h, :, cs]
                gate = gate + jnp.where(s1_ref[h, :, cs] >= th_rows[h][:, cs], pr, 0.0)
            a = act_ref[rs, cs]
            gelu = 0.5 * a * (1.0 + lax.erf(a * (2.0 ** -0.5)))
            coef_ref[rs, cs] = (gate * gelu).astype(BF16)
    acc_ref[...] += jnp.dot(wd_ref[...], coef_ref[...], preferred_element_type=F32)

    @pl.when(e == ne - 1)
    def _():
        o_ref[...] = x_ref[...] + acc_ref[...].T


def peer_experts(x, nw, wu, wdt, stats):
    t, d = x.shape
    tb, eb = PEER_TBK, PEER_EB
    ne = wu.shape[0] // eb
    return pl.pallas_call(
        _peer_expert_kernel,
        grid=(t // tb, ne),
        in_specs=[pl.BlockSpec((tb, d), lambda i, e: (i, 0)),
                  pl.BlockSpec((1, d), lambda i, e: (0, 0)),
                  pl.BlockSpec((eb, d), lambda i, e: (e, 0)),
                  pl.BlockSpec((d, eb), lambda i, e: (0, e))]
                 + [pl.BlockSpec((PEER_HEADS, N_KEYS, tb), lambda i, e: (0, 0, i))] * 4,
        out_specs=pl.BlockSpec((tb, d), lambda i, e: (i, 0)),
        out_shape=jax.ShapeDtypeStruct((t, d), F32),
        scratch_shapes=[pltpu.VMEM((d, tb), BF16),
                        pltpu.VMEM((d, tb), F32),
                        pltpu.VMEM((eb, tb), F32),
                        pltpu.VMEM((eb, tb), BF16)],
        compiler_params=_cparams(("parallel", "arbitrary")),
        name="peer_experts",
    )(x, nw.reshape(1, d), wu, wdt, *stats)


def _pad_rows(x, mult):
    t = x.shape[0]
    pad = (-t) % mult
    return x if pad == 0 else jnp.pad(x, ((0, pad), (0, 0)))


def _pick(t, pref):
    return pref if t % pref == 0 else t


def _peer(x1, wts):
    t = x1.shape[0]
    xp = _pad_rows(x1, PEER_TBK)
    qp = norm_matmul(xp, wts["norm2_w"], wts["peer_w_q"], _pick(xp.shape[0], 1024), 512)
    stats = peer_stats(qp, wts["peer_keys"])
    out = peer_experts(xp, wts["norm2_w"], wts["peer_w_up"], wts["peer_w_down_t"], stats)
    return out[:t]


def _layer(x, mem_k, mem_v, conv_prev, ssm_prev, wts, sb_fn, mem_tiled):
    nb, seq, d = x.shape
    t = nb * seq
    xf = x.reshape(t, d)
    yin = norm_matmul(xf, wts["norm1_w"], wts["w_in"], _pick(t, 1024), 512)
    y_ssm, ssm_new = ssd(yin, conv_prev, wts["conv_wg"], wts["ssd_hp"], wts["ssd_hx"], ssm_prev, nb, seq)
    o_sb = sb_fn(yin)
    o_mem = mem_attend(yin, mem_k, mem_v, wts["q_norm_w"], nb, seq, _pick(seq, 512), mem_tiled)
    x1 = merge(xf, yin, y_ssm, o_sb, o_mem, wts["w_ssm_out"], wts["w_sb_out"], wts["w_mem_out"], wts["w_out"],
               _pick(t, 256))
    y = _peer(x1, wts)
    y3 = yin.reshape(nb, seq, N_IN)
    k_new = y3[:, :, K0:K0 + d].reshape(nb, seq, SB_HEADS, SB_HEAD_DIM)
    v_new = y3[:, :, V0:V0 + d].reshape(nb, seq, SB_HEADS, SB_HEAD_DIM)
    xbc_raw = y3[:, :, XS0:XS0 + CONV_DIM]
    return y.reshape(nb, seq, d), k_new, v_new, xbc_raw, ssm_new


def _group_cols(a):
    lead = a.shape[:-1]
    xs = a[..., :D_INNER].reshape(lead + (SSM_GROUPS, GROUP_W))
    bm = a[..., D_INNER:D_INNER + SSM_GROUPS * D_STATE].reshape(lead + (SSM_GROUPS, D_STATE))
    cm = a[..., D_INNER + SSM_GROUPS * D_STATE:].reshape(lead + (SSM_GROUPS, D_STATE))
    return jnp.concatenate([xs, bm, cm], axis=-1)


def _prev_rows(prev):
    g = jnp.moveaxis(_group_cols(prev), 2, 1)
    return jnp.pad(g, ((0, 0), (0, 0), (8 - (CONV_W - 1), 0), (0, 0)))


def _prep_weights(l, norm1_w, w_in, conv_w, conv_b, dt_bias, a_log, d_skip, ssm_norm_w, w_ssm_out, w_sb_out,
                  sb_bias, mem_norm_w, w_mem_kv, q_norm_w, k_norm_w, w_mem_out, w_out, norm2_w, peer_w_q,
                  peer_sub_keys, peer_w_up, peer_w_down):
    hpg = SSM_HEADS // SSM_GROUPS
    w = w_in[l]
    dt_cols = w[:, D_INNER + CONV_DIM:D_INNER + CONV_DIM + SSM_HEADS]
    rest = w[:, D_INNER + CONV_DIM + SSM_HEADS:]
    dtc = jnp.pad(dt_cols.reshape(D_MODEL, SSM_GROUPS, hpg), ((0, 0), (0, 0), (0, LANES - hpg)))
    w_fused = jnp.concatenate([w[:, :D_INNER + CONV_DIM], rest, dtc.reshape(D_MODEL, SSM_GROUPS * LANES),
                               jnp.repeat(dt_cols, SSM_HEAD_DIM, axis=1)], axis=1).astype(BF16)
    assert w_fused.shape[1] == N_IN

    def compact(v):
        return jnp.pad(v.reshape(SSM_GROUPS, hpg), ((0, 0), (0, LANES - hpg)))

    def expand(v):
        return jnp.repeat(v, SSM_HEAD_DIM).reshape(SSM_GROUPS, GROUP_W)

    zc = jnp.zeros((SSM_GROUPS, LANES), F32)
    zx = jnp.zeros((SSM_GROUPS, GROUP_W), F32)
    ssd_hp = jnp.stack([compact(dt_bias[l]), compact(a_log[l])] + [zc] * 6, axis=1)
    ssd_hx = jnp.stack([expand(dt_bias[l]), expand(a_log[l]), expand(d_skip[l]),
                        ssm_norm_w[l].reshape(SSM_GROUPS, GROUP_W)] + [zx] * 4, axis=1)
    conv_rows = jnp.concatenate([conv_w[l], conv_b[l][None]], axis=0)
    conv_wg = jnp.pad(jnp.moveaxis(_group_cols(conv_rows), 1, 0), ((0, 0), (0, 8 - (CONV_W + 1)), (0, 0)))
    return dict(
        norm1_w=norm1_w[l], w_in=w_fused, conv_wg=conv_wg, ssd_hp=ssd_hp, ssd_hx=ssd_hx,
        w_ssm_out=w_ssm_out[l].astype(BF16), w_sb_out=w_sb_out[l].astype(BF16), sb_bias=sb_bias[l],
        mem_norm_w=mem_norm_w[l], w_mem_kv=w_mem_kv[l].astype(BF16), q_norm_w=q_norm_w[l],
        k_norm_w=k_norm_w[l], w_mem_out=w_mem_out[l].astype(BF16), w_out=w_out[l].astype(BF16),
        norm2_w=norm2_w[l], peer_w_q=peer_w_q[l].astype(BF16), peer_keys=peer_sub_keys[l].astype(BF16),
        peer_w_up=peer_w_up[l].astype(BF16), peer_w_down_t=peer_w_down[l].T.astype(BF16))


def kernel(x_prompt, x_sample, cache_sb_k, cache_sb_v, cache_mem_k, cache_mem_v, state_conv, state_ssm, page_table, mem_prompt, norm1_w, w_in, conv_w, conv_b, dt_bias, a_log, d_skip, ssm_norm_w, w_ssm_out, w_sb_out, sb_bias, mem_norm_w, w_mem_kv, q_norm_w, k_norm_w, w_mem_out, w_out, norm2_w, peer_w_q, peer_sub_keys, peer_w_up, peer_w_down):
    bp, seq_p, d = x_prompt.shape
    bs, seq_s, _ = x_sample.shape
    depth = w_in.shape[0]
    mt = mem_prompt.shape[1]
    mem_w = MEM_HEADS * MEM_HEAD_DIM
    y_p, y_s = x_prompt, x_sample
    outs = [[] for _ in range(10)]
    for l in range(depth):
        wts = _prep_weights(l, norm1_w, w_in, conv_w, conv_b, dt_bias, a_log, d_skip, ssm_norm_w, w_ssm_out,
                            w_sb_out, sb_bias, mem_norm_w, w_mem_kv, q_norm_w, k_norm_w, w_mem_out, w_out,
                            norm2_w, peer_w_q, peer_sub_keys, peer_w_up, peer_w_down)
        kv = norm_matmul(mem_prompt.reshape(bp * mt, d), wts["mem_norm_w"], wts["w_mem_kv"],
                         _pick(bp * mt, 1024), 512)
        mk = headnorm(kv, wts["k_norm_w"], _pick(bp * mt, 512)).reshape(bp, mt, mem_w)
        mv = kv[:, mem_w:].reshape(bp, mt, mem_w)
        prev0 = jnp.zeros((bp, SSM_GROUPS, 8, GROUP_W + 2 * D_STATE), F32)
        y_p, kp, vp, xbc_p, sp = _layer(
            y_p, mk, mv, prev0, None, wts,
            lambda yin: sb_prompt(yin, wts["sb_bias"], bp, seq_p), False)
        y_s, ks, vs, xbc_s, ss = _layer(
            y_s, _mem_tiled_view(cache_mem_k[l]), _mem_tiled_view(cache_mem_v[l]),
            _prev_rows(state_conv[l]), state_ssm[l], wts,
            lambda yin: sb_sample(yin, wts["sb_bias"], cache_sb_k[l], cache_sb_v[l], page_table, bs, seq_s), True)
        conv_s = jnp.concatenate([state_conv[l], xbc_s], axis=1)[:, -(CONV_W - 1):]
        for lst, v in zip(outs, [kp, vp, mk.reshape(bp, mt, MEM_HEADS, MEM_HEAD_DIM),
                                 mv.reshape(bp, mt, MEM_HEADS, MEM_HEAD_DIM), xbc_p[:, -(CONV_W - 1):], sp,
                                 ks, vs, conv_s, ss]):
            lst.append(v)
    return (y_p, y_s) + tuple(jnp.stack(o) for o in outs)
```

```python
import functools

import jax
import jax.numpy as jnp
from jax import lax
from jax.experimental import pallas as pl
from jax.experimental.pallas import tpu as pltpu

F32 = jnp.float32
BF16 = jnp.bfloat16

D_MODEL = 1024
D_INNER = 2048
SSM_HEADS = 32
SSM_HEAD_DIM = 64
SSM_GROUPS = 4
GROUP_W = D_INNER // SSM_GROUPS
D_STATE = 128
CONV_W = 4
CONV_DIM = D_INNER + 2 * SSM_GROUPS * D_STATE
SB_HEADS = 16
SB_HEAD_DIM = 64
MEM_HEADS = 4
MEM_HEAD_DIM = 256
PEER_HEADS = 8
N_KEYS = 128
PEER_TOPK = 16
RMS_EPS = 1e-6
NEG = -1e30

LANES = 128
CHUNK = 128
SSD_SHORT = 32
VMEM_LIMIT = 56 * 1024 * 1024

Z0 = 0
XS0 = Z0 + D_INNER
B0 = XS0 + D_INNER
C0 = B0 + SSM_GROUPS * D_STATE
Q0 = C0 + SSM_GROUPS * D_STATE
K0 = Q0 + D_MODEL
V0 = K0 + D_MODEL
QM0 = V0 + D_MODEL
G0 = QM0 + D_MODEL
DTC0 = G0 + 3 * D_MODEL
DTX0 = DTC0 + SSM_GROUPS * LANES
N_IN = DTX0 + D_INNER


def _cparams(sem):
    return pltpu.CompilerParams(dimension_semantics=sem, vmem_limit_bytes=VMEM_LIMIT)


def _softplus(v):
    return jnp.maximum(v, 0.0) + jnp.log(1.0 + jnp.exp(-jnp.abs(v)))


def _log_sigmoid(v):
    return jnp.minimum(v, 0.0) - jnp.log(1.0 + jnp.exp(-jnp.abs(v)))


def _split_bf16(v):
    hi = v.astype(BF16)
    lo = (v - hi.astype(F32)).astype(BF16)
    return hi, lo


def _norm_matmul_kernel(x_ref, nw_ref, w_ref, o_ref, h_ref):
    @pl.when(pl.program_id(1) == 0)
    def _():
        x = x_ref[...]
        ms = jnp.mean(x * x, axis=-1, keepdims=True)
        h_ref[...] = (x * lax.rsqrt(ms + RMS_EPS) * nw_ref[...]).astype(BF16)

    o_ref[...] = jnp.dot(h_ref[...], w_ref[...], preferred_element_type=F32)


def norm_matmul(x, nw, w, tm, tn):
    t, d = x.shape
    n = w.shape[1]
    return pl.pallas_call(
        _norm_matmul_kernel,
        grid=(t // tm, n // tn),
        in_specs=[pl.BlockSpec((tm, d), lambda i, j: (i, 0)),
                  pl.BlockSpec((1, d), lambda i, j: (0, 0)),
                  pl.BlockSpec((d, tn), lambda i, j: (0, j))],
        out_specs=pl.BlockSpec((tm, tn), lambda i, j: (i, j)),
        out_shape=jax.ShapeDtypeStruct((t, n), F32),
        scratch_shapes=[pltpu.VMEM((tm, d), BF16)],
        compiler_params=_cparams(("parallel", "arbitrary")),
        name="norm_matmul",
    )(x, nw.reshape(1, d), w)


def _headnorm_kernel(x_ref, w_ref, o_ref):
    for h in range(MEM_HEADS):
        sl = slice(h * MEM_HEAD_DIM, (h + 1) * MEM_HEAD_DIM)
        x = x_ref[:, sl]
        ms = jnp.mean(x * x, axis=-1, keepdims=True)
        o_ref[:, sl] = x * lax.rsqrt(ms + RMS_EPS) * w_ref[...]


def headnorm(kv, w, tm):
    t = kv.shape[0]
    width = MEM_HEADS * MEM_HEAD_DIM
    return pl.pallas_call(
        _headnorm_kernel,
        grid=(t // tm,),
        in_specs=[pl.BlockSpec((tm, width), lambda i: (i, 0)),
                  pl.BlockSpec((1, MEM_HEAD_DIM), lambda i: (0, 0))],
        out_specs=pl.BlockSpec((tm, width), lambda i: (i, 0)),
        out_shape=jax.ShapeDtypeStruct((t, width), F32),
        compiler_params=_cparams(("parallel",)),
        name="headnorm",
    )(kv, w.reshape(1, MEM_HEAD_DIM))


def _cumsum_rows(x):
    n = x.shape[0]
    r = lax.broadcasted_iota(jnp.int32, (n, 1), 0)
    s = 1
    while s < n:
        x = x + jnp.where(r >= s, pltpu.roll(x, s, axis=0), 0.0)
        s *= 2
    return x


def _ssd_kernel(*refs, q, qr, nc, has_init):
    if has_init:
        (xs_ref, b_ref, c_ref, z_ref, dtc_ref, dtx_ref, prev_ref, cw_ref, hp_ref, hx_ref, s0_ref,
         y_ref, so_ref, buf_ref, tail_ref, st_ref, dtb_ref) = refs
    else:
        (xs_ref, b_ref, c_ref, z_ref, dtc_ref, dtx_ref, prev_ref, cw_ref, hp_ref, hx_ref,
         y_ref, so_ref, buf_ref, tail_ref, st_ref, dtb_ref) = refs
        s0_ref = None
    cw_w = GROUP_W + 2 * D_STATE
    c = pl.program_id(2)

    @pl.when(c == 0)
    def _():
        tail_ref[...] = prev_ref[...]
        if has_init:
            st_ref[...] = s0_ref[...].reshape(GROUP_W, D_STATE).T
        else:
            st_ref[...] = jnp.zeros_like(st_ref)

    buf_ref[0:8, :] = tail_ref[...]
    if qr < q:
        buf_ref[8:, :] = jnp.zeros((q, cw_w), F32)
        dtb_ref[...] = jnp.zeros_like(dtb_ref)
    buf_ref[8:8 + qr, 0:GROUP_W] = xs_ref[...]
    buf_ref[8:8 + qr, GROUP_W:GROUP_W + D_STATE] = b_ref[...]
    buf_ref[8:8 + qr, GROUP_W + D_STATE:cw_w] = c_ref[...]
    dtb_ref[0:qr, 0:LANES] = dtc_ref[...]
    dtb_ref[0:qr, LANES:] = dtx_ref[...]
    if nc > 1:
        tail_ref[...] = buf_ref[q:q + 8, :]

    acc = jnp.broadcast_to(cw_ref[CONV_W:CONV_W + 1, :], (q, cw_w))
    for w in range(CONV_W):
        acc = acc + cw_ref[w:w + 1, :] * buf_ref[5 + w:5 + w + q, :]
    xbc = acc * jax.nn.sigmoid(acc)
    xs = xbc[:, 0:GROUP_W]
    bm = xbc[:, GROUP_W:GROUP_W + D_STATE]
    cm = xbc[:, GROUP_W + D_STATE:cw_w]

    rows = lax.broadcasted_iota(jnp.int32, (q, 1), 0)
    dtc = _softplus(dtb_ref[:, 0:LANES] + hp_ref[0:1, :])
    dtx = _softplus(dtb_ref[:, LANES:] + hx_ref[0:1, :])
    if qr < q:
        dtc = jnp.where(rows < qr, dtc, 0.0)
        dtx = jnp.where(rows < qr, dtx, 0.0)
    csc = _cumsum_rows(dtc * (-jnp.exp(hp_ref[1:2, :])))
    csx = _cumsum_rows(dtx * (-jnp.exp(hx_ref[1:2, :])))
    cst = csc.T
    cb = lax.dot_general(cm.astype(BF16), bm.astype(BF16), (((1,), (1,)), ((), ())),
                         preferred_element_type=F32)
    tri = rows >= lax.broadcasted_iota(jnp.int32, (1, q), 1)
    tot = csx[q - 1:q, :]
    xdt = xs * dtx
    wgt = xdt * jnp.exp(tot - csx)
    ecs = jnp.exp(csc)
    st = st_ref[...]
    half = lax.broadcasted_iota(jnp.int32, (1, LANES), 1) // SSM_HEAD_DIM
    ys = []
    for pr in range(GROUP_W // LANES):
        xp = xdt[:, pr * LANES:(pr + 1) * LANES]
        sp = st[:, pr * LANES:(pr + 1) * LANES]
        ya = jnp.zeros((q, LANES), F32)
        for hh in range(LANES // SSM_HEAD_DIM):
            r = pr * (LANES // SSM_HEAD_DIM) + hh
            seg = csc[:, r:r + 1] - cst[r:r + 1, :]
            m = (cb * jnp.exp(jnp.where(tri, seg, NEG))).astype(BF16)
            ce = (cm * ecs[:, r:r + 1]).astype(BF16)
            xm = jnp.where(half == hh, xp, 0.0).astype(BF16)
            sm = jnp.where(half == hh, sp, 0.0).astype(BF16)
            ya = ya + jnp.dot(m, xm, preferred_element_type=F32)
            ya = ya + jnp.dot(ce, sm, preferred_element_type=F32)
        ys.append(ya)
    y = jnp.concatenate(ys, axis=1) + hx_ref[2:3, :] * xs

    new_st = st * jnp.exp(tot) + jnp.dot(bm.T.astype(BF16), wgt.astype(BF16), preferred_element_type=F32)
    st_ref[...] = new_st

    @pl.when(c == nc - 1)
    def _():
        so_ref[...] = new_st.T.reshape(GROUP_W // SSM_HEAD_DIM, SSM_HEAD_DIM, D_STATE)

    z = z_ref[...]
    g = y[0:qr] * (z * jax.nn.sigmoid(z))
    ms = jnp.mean(g * g, axis=-1, keepdims=True)
    y_ref[...] = g * lax.rsqrt(ms + RMS_EPS) * hx_ref[3:4, :]


def ssd(yin, prev, cw, hp, hx, s0, nb, seq):
    if seq % CHUNK == 0:
        q, qr, nc = CHUNK, CHUNK, seq // CHUNK
    else:
        assert seq <= SSD_SHORT and seq % 8 == 0
        q, qr, nc = SSD_SHORT, seq, 1
    has_init = s0 is not None
    cw_w = GROUP_W + 2 * D_STATE
    hpg = SSM_HEADS // SSM_GROUPS

    def row(b, g, c):
        return b * nc + c

    in_specs = [
        pl.BlockSpec((qr, GROUP_W), lambda b, g, c: (row(b, g, c), XS0 // GROUP_W + g)),
        pl.BlockSpec((qr, D_STATE), lambda b, g, c: (row(b, g, c), B0 // D_STATE + g)),
        pl.BlockSpec((qr, D_STATE), lambda b, g, c: (row(b, g, c), C0 // D_STATE + g)),
        pl.BlockSpec((qr, GROUP_W), lambda b, g, c: (row(b, g, c), Z0 // GROUP_W + g)),
        pl.BlockSpec((qr, LANES), lambda b, g, c: (row(b, g, c), DTC0 // LANES + g)),
        pl.BlockSpec((qr, GROUP_W), lambda b, g, c: (row(b, g, c), DTX0 // GROUP_W + g)),
        pl.BlockSpec((None, None, 8, cw_w), lambda b, g, c: (b, g, 0, 0)),
        pl.BlockSpec((None, 8, cw_w), lambda b, g, c: (g, 0, 0)),
        pl.BlockSpec((None, 8, LANES), lambda b, g, c: (g, 0, 0)),
        pl.BlockSpec((None, 8, GROUP_W), lambda b, g, c: (g, 0, 0)),
    ]
    args = [yin, yin, yin, yin, yin, yin, prev, cw, hp, hx]
    if has_init:
        in_specs.append(pl.BlockSpec((None, hpg, SSM_HEAD_DIM, D_STATE), lambda b, g, c: (b, g, 0, 0)))
        args.append(s0)
    return pl.pallas_call(
        functools.partial(_ssd_kernel, q=q, qr=qr, nc=nc, has_init=has_init),
        grid=(nb, SSM_GROUPS, nc),
        in_specs=in_specs,
        out_specs=[pl.BlockSpec((qr, GROUP_W), lambda b, g, c: (row(b, g, c), g)),
                   pl.BlockSpec((None, hpg, SSM_HEAD_DIM, D_STATE), lambda b, g, c: (b, g, 0, 0))],
        out_shape=[jax.ShapeDtypeStruct((nb * seq, D_INNER), F32),
                   jax.ShapeDtypeStruct((nb, SSM_HEADS, SSM_HEAD_DIM, D_STATE), F32)],
        scratch_shapes=[pltpu.VMEM((q + 8, cw_w), F32),
                        pltpu.VMEM((8, cw_w), F32),
                        pltpu.VMEM((D_STATE, GROUP_W), F32),
                        pltpu.VMEM((q, LANES + GROUP_W), F32)],
        compiler_params=_cparams(("parallel", "parallel", "arbitrary")),
        name="ssd",
    )(*args)


SB_KBLK = 256
SB_QBLK = 512


def _sb_prompt_kernel(bias_ref, q_ref, k_ref, v_ref, o_ref):
    hp = pl.program_id(1)
    qi = pl.program_id(2)
    tq = SB_QBLK
    kb = SB_KBLK
    scale = SB_HEAD_DIM ** -0.5
    lane = lax.broadcasted_iota(jnp.int32, (1, LANES), 1)
    q = q_ref[...] * scale
    qstack = jnp.concatenate([jnp.where(lane // SB_HEAD_DIM == hh, q, 0.0) for hh in range(2)],
                             axis=0).astype(BF16)
    bias = [bias_ref[2 * hp + hh] for hh in range(2)]
    upper = (lax.broadcasted_iota(jnp.int32, (kb, kb), 0) >
             lax.broadcasted_iota(jnp.int32, (kb, kb), 1)).astype(BF16)
    qpos = qi * tq + lax.broadcasted_iota(jnp.int32, (tq, 1), 0)
    kcol = lax.broadcasted_iota(jnp.int32, (1, kb), 1)

    def block(kblk, carry, masked):
        runs, acc = carry
        start = pl.multiple_of(kblk * kb, kb)
        k = k_ref[pl.ds(start, kb), :].astype(BF16)
        v = v_ref[pl.ds(start, kb), :]
        vis = (start + kcol) < qpos
        zz = lax.dot_general(qstack, k, (((1,), (1,)), ((), ())), preferred_element_type=F32)
        lss, lks = [], []
        for hh in range(2):
            z = zz[hh * tq:(hh + 1) * tq] + bias[hh]
            ls = _log_sigmoid(z)
            lk = ls - z
            if masked:
                lk = jnp.where(vis, lk, 0.0)
            lss.append(ls)
            lks.append(lk)
        parts = [_split_bf16(lk) for lk in lks]
        stack = jnp.concatenate([p[0] for p in parts] + [p[1] for p in parts], axis=0)
        later = jnp.dot(stack, upper, preferred_element_type=F32)
        ws, vms, new_runs = [], [], []
        for hh in range(2):
            w = jnp.exp(lss[hh] + later[hh * tq:(hh + 1) * tq] + later[(2 + hh) * tq:(3 + hh) * tq] + runs[hh])
            if masked:
                w = jnp.where(vis, w, 0.0)
            ws.append(w.astype(BF16))
            vms.append(jnp.where(lane // SB_HEAD_DIM == hh, v, 0.0).astype(BF16))
            new_runs.append(runs[hh] + jnp.sum(lks[hh], axis=1, keepdims=True))
        acc = acc + jnp.dot(jnp.concatenate(ws, axis=1), jnp.concatenate(vms, axis=0),
                            preferred_element_type=F32)
        return tuple(new_runs), acc

    zero = jnp.zeros((tq, 1), F32)
    first = (qi * tq) // kb
    carry = ((zero, zero), jnp.zeros((tq, LANES), F32))
    for m in reversed(range(tq // kb)):
        carry = block(first + m, carry, True)

    def body(i, carry):
        return block(first - 1 - i, carry, False)

    carry = lax.fori_loop(0, first, body, carry)
    o_ref[...] = carry[1]


def sb_prompt(yin, bias, nb, seq):
    assert seq % SB_KBLK == 0 and seq % SB_QBLK == 0
    nq = seq // SB_QBLK
    return pl.pallas_call(
        _sb_prompt_kernel,
        grid=(nb, SB_HEADS // 2, nq),
        in_specs=[pl.BlockSpec(memory_space=pltpu.SMEM),
                  pl.BlockSpec((SB_QBLK, LANES), lambda b, h, i: (b * nq + i, Q0 // LANES + h)),
                  pl.BlockSpec((seq, LANES), lambda b, h, i: (b, K0 // LANES + h)),
                  pl.BlockSpec((seq, LANES), lambda b, h, i: (b, V0 // LANES + h))],
        out_specs=pl.BlockSpec((SB_QBLK, LANES), lambda b, h, i: (b * nq + i, h)),
        out_shape=jax.ShapeDtypeStruct((nb * seq, SB_HEADS * SB_HEAD_DIM), F32),
        compiler_params=_cparams(("parallel", "parallel", "arbitrary")),
        name="sb_prompt",
    )(bias, yin, yin, yin)


SB_PAGES_PER_STEP = 8


def _sb_sample_kernel(pt_ref, bias_ref, q_ref, kn_ref, vn_ref, *refs, npp, dq):
    krefs = refs[:npp]
    vrefs = refs[npp:2 * npp]
    o_ref = refs[2 * npp]
    qbt_ref, run_ref, acc_ref = refs[2 * npp + 1:]
    j = pl.program_id(1)
    width = SB_HEADS * SB_HEAD_DIM
    hq = SB_HEADS * dq
    page = CHUNK
    blk = 2 * page
    nblk = npp // 2
    rowh = lax.broadcasted_iota(jnp.int32, (hq, 1), 0) // dq
    colh = lax.broadcasted_iota(jnp.int32, (1, width), 1) // SB_HEAD_DIM
    diag = rowh == colh
    upper = (lax.broadcasted_iota(jnp.int32, (blk, blk), 0) >
             lax.broadcasted_iota(jnp.int32, (blk, blk), 1)).astype(BF16)

    @pl.when(j == 0)
    def _():
        qs = q_ref[...] * (SB_HEAD_DIM ** -0.5)
        qrep = jnp.concatenate([qs] * SB_HEADS, axis=0)
        qb = jnp.where(diag, qrep, 0.0).astype(BF16)
        qbt_ref[...] = qb
        pad = jnp.zeros((page - dq, width), F32)
        kn = jnp.concatenate([kn_ref[...], pad], axis=0).astype(BF16)
        vnt = jnp.concatenate([vn_ref[...], pad], axis=0).T.astype(BF16)
        z = lax.dot_general(qb, kn, (((1,), (1,)), ((), ())), preferred_element_type=F32) + bias_ref[...]
        vis = (lax.broadcasted_iota(jnp.int32, (1, page), 1) <
               lax.broadcasted_iota(jnp.int32, (hq, 1), 0) % dq)
        ls = _log_sigmoid(z)
        lk = jnp.where(vis, ls - z, 0.0)
        hi, lo = _split_bf16(lk)
        later = (jnp.dot(hi, upper[:page, :page], preferred_element_type=F32) +
                 jnp.dot(lo, upper[:page, :page], preferred_element_type=F32))
        w = jnp.where(vis, jnp.exp(ls + later), 0.0)
        acc_ref[...] = lax.dot_general(vnt, w.astype(BF16), (((1,), (1,)), ((), ())),
                                       preferred_element_type=F32)
        run_ref[...] = jnp.sum(lk, axis=1, keepdims=True)

    kcat = jnp.concatenate([krefs[i][...].astype(BF16) for i in range(npp)], axis=1)
    z = jnp.dot(qbt_ref[...], kcat, preferred_element_type=F32) + bias_ref[...]
    ls = _log_sigmoid(z)
    lk = ls - z
    hi, lo = _split_bf16(lk)
    stack = jnp.concatenate([hi[:, b * blk:(b + 1) * blk] for b in range(nblk)] +
                            [lo[:, b * blk:(b + 1) * blk] for b in range(nblk)], axis=0)
    later = jnp.dot(stack, upper, preferred_element_type=F32)
    run = run_ref[...]
    ws = [None] * nblk
    for b in reversed(range(nblk)):
        cs = slice(b * blk, (b + 1) * blk)
        later_b = later[b * hq:(b + 1) * hq] + later[(nblk + b) * hq:(nblk + b + 1) * hq]
        ws[b] = jnp.exp(ls[:, cs] + later_b + run).astype(BF16)
        run = run + jnp.sum(lk[:, cs], axis=1, keepdims=True)
    run_ref[...] = run
    wcat = jnp.concatenate(ws, axis=1)
    vcat = jnp.concatenate([vrefs[i][...].astype(BF16) for i in range(npp)], axis=1)
    acc_ref[...] += lax.dot_general(vcat, wcat, (((1,), (1,)), ((), ())), preferred_element_type=F32)

    @pl.when(j == pl.num_programs(1) - 1)
    def _():
        sel = jnp.where(diag, acc_ref[...].T, 0.0)
        o = sel[0:dq]
        for h in range(1, SB_HEADS):
            o = o + sel[h * dq:(h + 1) * dq]
        o_ref[...] = o


def sb_sample(yin, bias, cache_k, cache_v, page_table, nb, dq):
    n_pool, page = cache_k.shape[0], cache_k.shape[1]
    assert page == CHUNK and dq == 8
    width = SB_HEADS * SB_HEAD_DIM
    npages = page_table.shape[1]
    npp = SB_PAGES_PER_STEP
    assert npages % npp == 0
    hq = SB_HEADS * dq
    ck = jnp.transpose(cache_k, (0, 2, 3, 1)).reshape(n_pool, width, page)
    cv = jnp.transpose(cache_v, (0, 2, 3, 1)).reshape(n_pool, width, page)
    bias_l = jnp.repeat(bias, dq).reshape(hq, 1)

    def page_spec(i):
        return pl.BlockSpec((None, width, page),
                            lambda b, j, pt: (pt[b, npages - (j + 1) * npp + i], 0, 0))

    grid_spec = pltpu.PrefetchScalarGridSpec(
        num_scalar_prefetch=1,
        grid=(nb, npages // npp),
        in_specs=[pl.BlockSpec((hq, 1), lambda b, j, pt: (0, 0)),
                  pl.BlockSpec((dq, width), lambda b, j, pt: (b, Q0 // width)),
                  pl.BlockSpec((dq, width), lambda b, j, pt: (b, K0 // width)),
                  pl.BlockSpec((dq, width), lambda b, j, pt: (b, V0 // width))]
                 + [page_spec(i) for i in range(npp)] + [page_spec(i) for i in range(npp)],
        out_specs=pl.BlockSpec((dq, width), lambda b, j, pt: (b, 0)),
        scratch_shapes=[pltpu.VMEM((hq, width), BF16),
                        pltpu.VMEM((hq, 1), F32),
                        pltpu.VMEM((width, hq), F32)],
    )
    return pl.pallas_call(
        functools.partial(_sb_sample_kernel, npp=npp, dq=dq),
        grid_spec=grid_spec,
        out_shape=jax.ShapeDtypeStruct((nb * dq, width), F32),
        compiler_params=_cparams(("parallel", "arbitrary")),
        name="sb_sample",
    )(page_table, bias_l, yin, yin, yin, *([ck] * npp), *([cv] * npp))


MEM_DCHUNKS = MEM_HEAD_DIM // LANES


def _mem_attend_kernel(q_ref, k_ref, v_ref, w_ref, o_ref, *, tiled):
    def head(ref, h):
        if not tiled:
            return ref[:, h * MEM_HEAD_DIM:(h + 1) * MEM_HEAD_DIM]
        rows = ref.shape[0] // (MEM_DCHUNKS * MEM_HEADS)
        return jnp.concatenate(
            [ref[pl.ds(dc * MEM_HEADS + h, rows, stride=MEM_DCHUNKS * MEM_HEADS), :] for dc in range(MEM_DCHUNKS)],
            axis=1)

    for h in range(MEM_HEADS):
        sl = slice(h * MEM_HEAD_DIM, (h + 1) * MEM_HEAD_DIM)
        q = q_ref[:, sl]
        ms = jnp.mean(q * q, axis=-1, keepdims=True)
        qn = (q * lax.rsqrt(ms + RMS_EPS) * w_ref[...]).astype(BF16)
        s = lax.dot_general(qn, head(k_ref, h).astype(BF16), (((1,), (1,)), ((), ())),
                            preferred_element_type=F32) * (MEM_HEAD_DIM ** -0.5)
        e = jnp.exp(s - jnp.max(s, axis=-1, keepdims=True))
        p = e / jnp.sum(e, axis=-1, keepdims=True)
        o_ref[:, sl] = jnp.dot(p.astype(BF16), head(v_ref, h).astype(BF16), preferred_element_type=F32)


def _mem_tiled_view(cache):
    nb, mt = cache.shape[0], cache.shape[1]
    v = cache.reshape(nb, mt, MEM_HEADS, MEM_DCHUNKS, LANES)
    return jnp.transpose(v, (0, 1, 3, 2, 4)).reshape(nb, mt * MEM_DCHUNKS * MEM_HEADS, LANES)


def mem_attend(yin, mem_k, mem_v, qnw, nb, seq, tq, tiled=False):
    width = MEM_HEADS * MEM_HEAD_DIM
    nq = seq // tq
    kv_block = (None,) + tuple(mem_k.shape[1:])
    return pl.pallas_call(
        functools.partial(_mem_attend_kernel, tiled=tiled),
        grid=(nb, nq),
        in_specs=[pl.BlockSpec((tq, width), lambda b, i: (b * nq + i, QM0 // width)),
                  pl.BlockSpec(kv_block, lambda b, i: (b, 0, 0)),
                  pl.BlockSpec(kv_block, lambda b, i: (b, 0, 0)),
                  pl.BlockSpec((1, MEM_HEAD_DIM), lambda b, i: (0, 0))],
        out_specs=pl.BlockSpec((tq, width), lambda b, i: (b * nq + i, 0)),
        out_shape=jax.ShapeDtypeStruct((nb * seq, width), F32),
        compiler_params=_cparams(("parallel", "arbitrary")),
        name="mem_attend",
    )(yin, mem_k, mem_v, qnw.reshape(1, MEM_HEAD_DIM))


def _merge_kernel(x_ref, y_ref, sb_ref, mm_ref, g1_ref, g2_ref, g3_ref, w1_ref, w2_ref, w3_ref, wo_ref, o_ref):
    b1 = jnp.dot(y_ref[...].astype(BF16), w1_ref[...], preferred_element_type=F32)
    b2 = jnp.dot(sb_ref[...].astype(BF16), w2_ref[...], preferred_element_type=F32)
    b3 = jnp.dot(mm_ref[...].astype(BF16), w3_ref[...], preferred_element_type=F32)
    merged = (jax.nn.sigmoid(g1_ref[...]) * b1 + jax.nn.sigmoid(g2_ref[...]) * b2 +
              jax.nn.sigmoid(g3_ref[...]) * b3)
    o_ref[...] = x_ref[...] + jnp.dot(merged.astype(BF16), wo_ref[...], preferred_element_type=F32)


def merge(x, yin, y_ssm, o_sb, o_mem, w1, w2, w3, wo, tm):
    t, d = x.shape

    def tok(width, col=0):
        return pl.BlockSpec((tm, width), lambda i: (i, col))

    def full(w):
        return pl.BlockSpec(w.shape, lambda i: (0, 0))

    return pl.pallas_call(
        _merge_kernel,
        grid=(t // tm,),
        in_specs=[tok(d), tok(D_INNER), tok(d), tok(d),
                  tok(d, G0 // d), tok(d, G0 // d + 1), tok(d, G0 // d + 2),
                  full(w1), full(w2), full(w3), full(wo)],
        out_specs=tok(d),
        out_shape=jax.ShapeDtypeStruct((t, d), F32),
        compiler_params=_cparams(("parallel",)),
        name="merge",
    )(x, y_ssm, o_sb, o_mem, yin, yin, yin, w1, w2, w3, wo)


PEER_TB = 256


def _top_values(s, n):
    out = []
    for _ in range(n):
        m = jnp.max(s, axis=0, keepdims=True)
        out.append(m)
        s = jnp.where(s == m, NEG, s)
    return out


def _peer_stats_kernel(q_ref, keys_ref, th_ref, p0_ref, s1_ref, p1_ref):
    tb = q_ref.shape[0]
    big = -NEG
    for h in range(PEER_HEADS):
        tops, scores = [], []
        for c in range(2):
            col = (h * 2 + c) * N_KEYS
            s = lax.dot_general(keys_ref[c], q_ref[:, col:col + N_KEYS].astype(BF16),
                                (((1,), (1,)), ((), ())), preferred_element_type=F32)
            scores.append(s)
            tops.append(_top_values(s, PEER_TOPK))
        b = jnp.concatenate(tops[1], axis=0)
        row8 = lax.broadcasted_iota(jnp.int32, (8, 1), 0)
        pieces = [tops[0][0] + b]
        for a in range(1, 8):
            part = tops[0][a] + b[0:8]
            if PEER_TOPK // (a + 1) < 8:
                part = jnp.where(row8 < PEER_TOPK // (a + 1), part, NEG)
            pieces.append(part)
        pieces.append(jnp.concatenate(tops[0][8:], axis=0) + tops[1][0])
        cand = jnp.concatenate(pieces, axis=0)
        best = _top_values(cand, PEER_TOPK)
        zsum = jnp.zeros((1, tb), F32)
        for m in best:
            zsum = zsum + jnp.exp(m - best[0])
        tau = best[-1]
        thetas = [jnp.min(jnp.where(pieces[0] >= tau, b, big), axis=0, keepdims=True)]
        for a in range(1, 8):
            thetas.append(jnp.min(jnp.where(pieces[a] >= tau, b[0:8], big), axis=0, keepdims=True))
        tail = jnp.where(pieces[8] >= tau, tops[1][0], big)
        thetas += [tail[r:r + 1] for r in range(8)]
        theta = jnp.full((N_KEYS, tb), big, F32)
        for a in range(PEER_TOPK):
            theta = jnp.where(scores[0] == tops[0][a], thetas[a], theta)
        th_ref[h] = theta
        p0_ref[h] = jnp.exp(scores[0] - tops[0][0])
        s1_ref[h] = scores[1]
        p1_ref[h] = jnp.exp(scores[1] - tops[1][0]) * (1.0 / zsum)


def peer_stats(qp, keys):
    t = qp.shape[0]
    tb = PEER_TB
    return pl.pallas_call(
        _peer_stats_kernel,
        grid=(t // tb,),
        in_specs=[pl.BlockSpec((tb, qp.shape[1]), lambda i: (i, 0)),
                  pl.BlockSpec(keys.shape, lambda i: (0, 0, 0))],
        out_specs=[pl.BlockSpec((PEER_HEADS, N_KEYS, tb), lambda i: (0, 0, i))] * 4,
        out_shape=[jax.ShapeDtypeStruct((PEER_HEADS, N_KEYS, t), F32)] * 4,
        compiler_params=_cparams(("parallel",)),
        name="peer_stats",
    )(qp, keys)


PEER_TBK = 512
PEER_EB = 512
PEER_SUB = 32


def _peer_expert_kernel(x_ref, nw_ref, wu_ref, wd_ref, th_ref, p0_ref, s1_ref, p1_ref, o_ref,
                        xt_ref, acc_ref, act_ref, coef_ref, rows_ref):
    e = pl.program_id(1)
    ne = pl.num_programs(1)
    nsub = PEER_EB // N_KEYS

    @pl.when(e == 0)
    def _():
        x = x_ref[...]
        ms = jnp.mean(x * x, axis=-1, keepdims=True)
        xt_ref[...] = (x * lax.rsqrt(ms + RMS_EPS) * nw_ref[...]).T.astype(BF16)
        acc_ref[...] = jnp.zeros_like(acc_ref)

    act_ref[...] = jnp.dot(wu_ref[...], xt_ref[...], preferred_element_type=F32)
    for ii in range(nsub):
        i = e * nsub + ii
        for h in range(PEER_HEADS):
            rows_ref[h:h + 1, :] = th_ref[h, pl.ds(i, 1), :]
            rows_ref[PEER_HEADS + h:PEER_HEADS + h + 1, :] = p0_ref[h, pl.ds(i, 1), :]
        for tc in range(x_ref.shape[0] // LANES):
            cs = slice(tc * LANES, (tc + 1) * LANES)
            for sub in range(N_KEYS // PEER_SUB):
                js = slice(sub * PEER_SUB, (sub + 1) * PEER_SUB)
                es = slice(ii * N_KEYS + sub * PEER_SUB, ii * N_KEYS + (sub + 1) * PEER_SUB)
                gate = jnp.zeros((PEER_SUB, LANES), F32)
                for h in range(PEER_HEADS):
                    pr = rows_ref[PEER_HEADS + h:PEER_HEADS + h + 1, cs] * p1_ref[h, js, cs]
                    gate = gate + jnp.where(s1_ref[h, js, cs] >= rows_ref[h:h + 1, cs], pr, 0.0)
                a = act_ref[es, cs]
                gelu = 0.5 * a * (1.0 + lax.erf(a * (2.0 ** -0.5)))
                coef_ref[es, cs] = (gate * gelu).astype(BF16)
    acc_ref[...] += jnp.dot(wd_ref[...], coef_ref[...], preferred_element_type=F32)

    @pl.when(e == ne - 1)
    def _():
        o_ref[...] = x_ref[...] + acc_ref[...].T


def peer_experts(x, nw, wu, wdt, stats):
    t, d = x.shape
    tb, eb = PEER_TBK, PEER_EB
    ne = wu.shape[0] // eb
    return pl.pallas_call(
        _peer_expert_kernel,
        grid=(t // tb, ne),
        in_specs=[pl.BlockSpec((tb, d), lambda i, e: (i, 0)),
                  pl.BlockSpec((1, d), lambda i, e: (0, 0)),
                  pl.BlockSpec((eb, d), lambda i, e: (e, 0)),
                  pl.BlockSpec((d, eb), lambda i, e: (0, e))]
                 + [pl.BlockSpec((PEER_HEADS, N_KEYS, tb), lambda i, e: (0, 0, i))] * 4,
        out_specs=pl.BlockSpec((tb, d), lambda i, e: (i, 0)),
        out_shape=jax.ShapeDtypeStruct((t, d), F32),
        scratch_shapes=[pltpu.VMEM((d, tb), BF16),
                        pltpu.VMEM((d, tb), F32),
                        pltpu.VMEM((eb, tb), F32),
                        pltpu.VMEM((eb, tb), BF16),
                        pltpu.VMEM((2 * PEER_HEADS, tb), F32)],
        compiler_params=_cparams(("parallel", "arbitrary")),
        name="peer_experts",
    )(x, nw.reshape(1, d), wu, wdt, *stats)


def _pad_rows(x, mult):
    t = x.shape[0]
    pad = (-t) % mult
    return x if pad == 0 else jnp.pad(x, ((0, pad), (0, 0)))


def _pick(t, pref):
    return pref if t % pref == 0 else t


def _peer(x1, wts):
    t = x1.shape[0]
    xp = _pad_rows(x1, PEER_TBK)
    qp = norm_matmul(xp, wts["norm2_w"], wts["peer_w_q"], _pick(xp.shape[0], 1024), 512)
    stats = peer_stats(qp, wts["peer_keys"])
    out = peer_experts(xp, wts["norm2_w"], wts["peer_w_up"], wts["peer_w_down_t"], stats)
    return out[:t]


def _layer(x, mem_k, mem_v, conv_prev, ssm_prev, wts, sb_fn, mem_tiled):
    nb, seq, d = x.shape
    t = nb * seq
    xf = x.reshape(t, d)
    yin = norm_matmul(xf, wts["norm1_w"], wts["w_in"], _pick(t, 2048), 512)
    y_ssm, ssm_new = ssd(yin, conv_prev, wts["conv_wg"], wts["ssd_hp"], wts["ssd_hx"], ssm_prev, nb, seq)
    o_sb = sb_fn(yin)
    o_mem = mem_attend(yin, mem_k, mem_v, wts["q_norm_w"], nb, seq, _pick(seq, 512), mem_tiled)
    x1 = merge(xf, yin, y_ssm, o_sb, o_mem, wts["w_ssm_out"], wts["w_sb_out"], wts["w_mem_out"], wts["w_out"],
               _pick(t, 256))
    y = _peer(x1, wts)
    y3 = yin.reshape(nb, seq, N_IN)
    k_new = y3[:, :, K0:K0 + d].reshape(nb, seq, SB_HEADS, SB_HEAD_DIM)
    v_new = y3[:, :, V0:V0 + d].reshape(nb, seq, SB_HEADS, SB_HEAD_DIM)
    xbc_raw = y3[:, :, XS0:XS0 + CONV_DIM]
    return y.reshape(nb, seq, d), k_new, v_new, xbc_raw, ssm_new


def _group_cols(a):
    lead = a.shape[:-1]
    xs = a[..., :D_INNER].reshape(lead + (SSM_GROUPS, GROUP_W))
    bm = a[..., D_INNER:D_INNER + SSM_GROUPS * D_STATE].reshape(lead + (SSM_GROUPS, D_STATE))
    cm = a[..., D_INNER + SSM_GROUPS * D_STATE:].reshape(lead + (SSM_GROUPS, D_STATE))
    return jnp.concatenate([xs, bm, cm], axis=-1)


def _prev_rows(prev):
    g = jnp.moveaxis(_group_cols(prev), 2, 1)
    return jnp.pad(g, ((0, 0), (0, 0), (8 - (CONV_W - 1), 0), (0, 0)))


def _prep_weights(l, norm1_w, w_in, conv_w, conv_b, dt_bias, a_log, d_skip, ssm_norm_w, w_ssm_out, w_sb_out,
                  sb_bias, mem_norm_w, w_mem_kv, q_norm_w, k_norm_w, w_mem_out, w_out, norm2_w, peer_w_q,
                  peer_sub_keys, peer_w_up, peer_w_down):
    hpg = SSM_HEADS // SSM_GROUPS
    w = w_in[l]
    dt_cols = w[:, D_INNER + CONV_DIM:D_INNER + CONV_DIM + SSM_HEADS]
    rest = w[:, D_INNER + CONV_DIM + SSM_HEADS:]
    dtc = jnp.pad(dt_cols.reshape(D_MODEL, SSM_GROUPS, hpg), ((0, 0), (0, 0), (0, LANES - hpg)))
    w_fused = jnp.concatenate([w[:, :D_INNER + CONV_DIM], rest, dtc.reshape(D_MODEL, SSM_GROUPS * LANES),
                               jnp.repeat(dt_cols, SSM_HEAD_DIM, axis=1)], axis=1).astype(BF16)
    assert w_fused.shape[1] == N_IN

    def compact(v):
        return jnp.pad(v.reshape(SSM_GROUPS, hpg), ((0, 0), (0, LANES - hpg)))

    def expand(v):
        return jnp.repeat(v, SSM_HEAD_DIM).reshape(SSM_GROUPS, GROUP_W)

    zc = jnp.zeros((SSM_GROUPS, LANES), F32)
    zx = jnp.zeros((SSM_GROUPS, GROUP_W), F32)
    ssd_hp = jnp.stack([compact(dt_bias[l]), compact(a_log[l])] + [zc] * 6, axis=1)
    ssd_hx = jnp.stack([expand(dt_bias[l]), expand(a_log[l]), expand(d_skip[l]),
                        ssm_norm_w[l].reshape(SSM_GROUPS, GROUP_W)] + [zx] * 4, axis=1)
    conv_rows = jnp.concatenate([conv_w[l], conv_b[l][None]], axis=0)
    conv_wg = jnp.pad(jnp.moveaxis(_group_cols(conv_rows), 1, 0), ((0, 0), (0, 8 - (CONV_W + 1)), (0, 0)))
    return dict(
        norm1_w=norm1_w[l], w_in=w_fused, conv_wg=conv_wg, ssd_hp=ssd_hp, ssd_hx=ssd_hx,
        w_ssm_out=w_ssm_out[l].astype(BF16), w_sb_out=w_sb_out[l].astype(BF16), sb_bias=sb_bias[l],
        mem_norm_w=mem_norm_w[l], w_mem_kv=w_mem_kv[l].astype(BF16), q_norm_w=q_norm_w[l],
        k_norm_w=k_norm_w[l], w_mem_out=w_mem_out[l].astype(BF16), w_out=w_out[l].astype(BF16),
        norm2_w=norm2_w[l], peer_w_q=peer_w_q[l].astype(BF16), peer_keys=peer_sub_keys[l].astype(BF16),
        peer_w_up=peer_w_up[l].astype(BF16), peer_w_down_t=peer_w_down[l].T.astype(BF16))


def kernel(x_prompt, x_sample, cache_sb_k, cache_sb_v, cache_mem_k, cache_mem_v, state_conv, state_ssm, page_table, mem_prompt, norm1_w, w_in, conv_w, conv_b, dt_bias, a_log, d_skip, ssm_norm_w, w_ssm_out, w_sb_out, sb_bias, mem_norm_w, w_mem_kv, q_norm_w, k_norm_w, w_mem_out, w_out, norm2_w, peer_w_q, peer_sub_keys, peer_w_up, peer_w_down):
    bp, seq_p, d = x_prompt.shape
    bs, seq_s, _ = x_sample.shape
    depth = w_in.shape[0]
    mt = mem_prompt.shape[1]
    mem_w = MEM_HEADS * MEM_HEAD_DIM
    y_p, y_s = x_prompt, x_sample
    outs = [[] for _ in range(10)]
    for l in range(depth):
        wts = _prep_weights(l, norm1_w, w_in, conv_w, conv_b, dt_bias, a_log, d_skip, ssm_norm_w, w_ssm_out,
                            w_sb_out, sb_bias, mem_norm_w, w_mem_kv, q_norm_w, k_norm_w, w_mem_out, w_out,
                            norm2_w, peer_w_q, peer_sub_keys, peer_w_up, peer_w_down)
        kv = norm_matmul(mem_prompt.reshape(bp * mt, d), wts["mem_norm_w"], wts["w_mem_kv"],
                         _pick(bp * mt, 1024), 512)
        mk = headnorm(kv, wts["k_norm_w"], _pick(bp * mt, 512)).reshape(bp, mt, mem_w)
        mv = kv[:, mem_w:].reshape(bp, mt, mem_w)
        prev0 = jnp.zeros((bp, SSM_GROUPS, 8, GROUP_W + 2 * D_STATE), F32)
        y_p, kp, vp, xbc_p, sp = _layer(
            y_p, mk, mv, prev0, None, wts,
            lambda yin: sb_prompt(yin, wts["sb_bias"], bp, seq_p), False)
        y_s, ks, vs, xbc_s, ss = _layer(
            y_s, _mem_tiled_view(cache_mem_k[l]), _mem_tiled_view(cache_mem_v[l]),
            _prev_rows(state_conv[l]), state_ssm[l], wts,
            lambda yin: sb_sample(yin, wts["sb_bias"], cache_sb_k[l], cache_sb_v[l], page_table, bs, seq_s), True)
        conv_s = jnp.concatenate([state_conv[l], xbc_s], axis=1)[:, -(CONV_W - 1):]
        for lst, v in zip(outs, [kp, vp, mk.reshape(bp, mt, MEM_HEADS, MEM_HEAD_DIM),
                                 mv.reshape(bp, mt, MEM_HEADS, MEM_HEAD_DIM), xbc_p[:, -(CONV_W - 1):], sp,
                                 ks, vs, conv_s, ss]):
            lst.append(v)
    return (y_p, y_s) + tuple(jnp.stack(o) for o in outs)
```

```python
import functools

import jax
import jax.numpy as jnp
from jax import lax
from jax.experimental import pallas as pl
from jax.experimental.pallas import tpu as pltpu

F32 = jnp.float32
BF16 = jnp.bfloat16

D_MODEL = 1024
D_INNER = 2048
SSM_HEADS = 32
SSM_HEAD_DIM = 64
SSM_GROUPS = 4
GROUP_W = D_INNER // SSM_GROUPS
D_STATE = 128
CONV_W = 4
CONV_DIM = D_INNER + 2 * SSM_GROUPS * D_STATE
SB_HEADS = 16
SB_HEAD_DIM = 64
MEM_HEADS = 4
MEM_HEAD_DIM = 256
PEER_HEADS = 8
N_KEYS = 128
PEER_TOPK = 16
RMS_EPS = 1e-6
NEG = -1e30

LANES = 128
CHUNK = 128
SSD_SHORT = 32
VMEM_LIMIT = 56 * 1024 * 1024

Z0 = 0
XS0 = Z0 + D_INNER
B0 = XS0 + D_INNER
C0 = B0 + SSM_GROUPS * D_STATE
Q0 = C0 + SSM_GROUPS * D_STATE
K0 = Q0 + D_MODEL
V0 = K0 + D_MODEL
QM0 = V0 + D_MODEL
G0 = QM0 + D_MODEL
DTC0 = G0 + 3 * D_MODEL
DTX0 = DTC0 + SSM_GROUPS * LANES
N_IN = DTX0 + D_INNER


def _cparams(sem):
    return pltpu.CompilerParams(dimension_semantics=sem, vmem_limit_bytes=VMEM_LIMIT)


def _softplus(v):
    return jnp.maximum(v, 0.0) + jnp.log(1.0 + jnp.exp(-jnp.abs(v)))


def _log_sigmoid(v):
    return jnp.minimum(v, 0.0) - jnp.log(1.0 + jnp.exp(-jnp.abs(v)))


def _split_bf16(v):
    hi = v.astype(BF16)
    lo = (v - hi.astype(F32)).astype(BF16)
    return hi, lo


def _norm_matmul_kernel(x_ref, nw_ref, w_ref, o_ref, h_ref):
    @pl.when(pl.program_id(1) == 0)
    def _():
        x = x_ref[...]
        ms = jnp.mean(x * x, axis=-1, keepdims=True)
        h_ref[...] = (x * lax.rsqrt(ms + RMS_EPS) * nw_ref[...]).astype(BF16)

    o_ref[...] = jnp.dot(h_ref[...], w_ref[...], preferred_element_type=F32)


def norm_matmul(x, nw, w, tm, tn):
    t, d = x.shape
    n = w.shape[1]
    return pl.pallas_call(
        _norm_matmul_kernel,
        grid=(t // tm, n // tn),
        in_specs=[pl.BlockSpec((tm, d), lambda i, j: (i, 0)),
                  pl.BlockSpec((1, d), lambda i, j: (0, 0)),
                  pl.BlockSpec((d, tn), lambda i, j: (0, j))],
        out_specs=pl.BlockSpec((tm, tn), lambda i, j: (i, j)),
        out_shape=jax.ShapeDtypeStruct((t, n), F32),
        scratch_shapes=[pltpu.VMEM((tm, d), BF16)],
        compiler_params=_cparams(("parallel", "arbitrary")),
        name="norm_matmul",
    )(x, nw.reshape(1, d), w)


def _headnorm_kernel(x_ref, w_ref, o_ref):
    for h in range(MEM_HEADS):
        sl = slice(h * MEM_HEAD_DIM, (h + 1) * MEM_HEAD_DIM)
        x = x_ref[:, sl]
        ms = jnp.mean(x * x, axis=-1, keepdims=True)
        o_ref[:, sl] = x * lax.rsqrt(ms + RMS_EPS) * w_ref[...]


def headnorm(kv, w, tm):
    t = kv.shape[0]
    width = MEM_HEADS * MEM_HEAD_DIM
    return pl.pallas_call(
        _headnorm_kernel,
        grid=(t // tm,),
        in_specs=[pl.BlockSpec((tm, width), lambda i: (i, 0)),
                  pl.BlockSpec((1, MEM_HEAD_DIM), lambda i: (0, 0))],
        out_specs=pl.BlockSpec((tm, width), lambda i: (i, 0)),
        out_shape=jax.ShapeDtypeStruct((t, width), F32),
        compiler_params=_cparams(("parallel",)),
        name="headnorm",
    )(kv, w.reshape(1, MEM_HEAD_DIM))


def _cumsum_rows(x):
    n = x.shape[0]
    r = lax.broadcasted_iota(jnp.int32, (n, 1), 0)
    s = 1
    while s < n:
        x = x + jnp.where(r >= s, pltpu.roll(x, s, axis=0), 0.0)
        s *= 2
    return x


def _ssd_kernel(*refs, q, qr, nc, has_init):
    if has_init:
        (xs_ref, b_ref, c_ref, z_ref, dtc_ref, dtx_ref, prev_ref, cw_ref, hp_ref, hx_ref, s0_ref,
         y_ref, so_ref, buf_ref, tail_ref, st_ref, dtb_ref) = refs
    else:
        (xs_ref, b_ref, c_ref, z_ref, dtc_ref, dtx_ref, prev_ref, cw_ref, hp_ref, hx_ref,
         y_ref, so_ref, buf_ref, tail_ref, st_ref, dtb_ref) = refs
        s0_ref = None
    cw_w = GROUP_W + 2 * D_STATE
    c = pl.program_id(2)

    @pl.when(c == 0)
    def _():
        tail_ref[...] = prev_ref[...]
        if has_init:
            st_ref[...] = s0_ref[...].reshape(GROUP_W, D_STATE).T
        else:
            st_ref[...] = jnp.zeros_like(st_ref)

    buf_ref[0:8, :] = tail_ref[...]
    if qr < q:
        buf_ref[8:, :] = jnp.zeros((q, cw_w), F32)
        dtb_ref[...] = jnp.zeros_like(dtb_ref)
    buf_ref[8:8 + qr, 0:GROUP_W] = xs_ref[...]
    buf_ref[8:8 + qr, GROUP_W:GROUP_W + D_STATE] = b_ref[...]
    buf_ref[8:8 + qr, GROUP_W + D_STATE:cw_w] = c_ref[...]
    dtb_ref[0:qr, 0:LANES] = dtc_ref[...]
    dtb_ref[0:qr, LANES:] = dtx_ref[...]
    if nc > 1:
        tail_ref[...] = buf_ref[q:q + 8, :]

    acc = jnp.broadcast_to(cw_ref[CONV_W:CONV_W + 1, :], (q, cw_w))
    for w in range(CONV_W):
        acc = acc + cw_ref[w:w + 1, :] * buf_ref[5 + w:5 + w + q, :]
    xbc = acc * jax.nn.sigmoid(acc)
    xs = xbc[:, 0:GROUP_W]
    bm = xbc[:, GROUP_W:GROUP_W + D_STATE]
    cm = xbc[:, GROUP_W + D_STATE:cw_w]

    rows = lax.broadcasted_iota(jnp.int32, (q, 1), 0)
    dtc = _softplus(dtb_ref[:, 0:LANES] + hp_ref[0:1, :])
    dtx = _softplus(dtb_ref[:, LANES:] + hx_ref[0:1, :])
    if qr < q:
        dtc = jnp.where(rows < qr, dtc, 0.0)
        dtx = jnp.where(rows < qr, dtx, 0.0)
    csc = _cumsum_rows(dtc * (-jnp.exp(hp_ref[1:2, :])))
    csx = _cumsum_rows(dtx * (-jnp.exp(hx_ref[1:2, :])))
    cst = csc.T
    cb = lax.dot_general(cm.astype(BF16), bm.astype(BF16), (((1,), (1,)), ((), ())),
                         preferred_element_type=F32)
    tri = rows >= lax.broadcasted_iota(jnp.int32, (1, q), 1)
    tot = csx[q - 1:q, :]
    xdt = xs * dtx
    wgt = xdt * jnp.exp(tot - csx)
    ecs = jnp.exp(csc)
    st = st_ref[...]
    half = lax.broadcasted_iota(jnp.int32, (1, LANES), 1) // SSM_HEAD_DIM
    ys = []
    for pr in range(GROUP_W // LANES):
        xp = xdt[:, pr * LANES:(pr + 1) * LANES]
        sp = st[:, pr * LANES:(pr + 1) * LANES]
        ya = jnp.zeros((q, LANES), F32)
        for hh in range(LANES // SSM_HEAD_DIM):
            r = pr * (LANES // SSM_HEAD_DIM) + hh
            seg = csc[:, r:r + 1] - cst[r:r + 1, :]
            m = (cb * jnp.exp(jnp.where(tri, seg, NEG))).astype(BF16)
            ce = (cm * ecs[:, r:r + 1]).astype(BF16)
            xm = jnp.where(half == hh, xp, 0.0).astype(BF16)
            sm = jnp.where(half == hh, sp, 0.0).astype(BF16)
            ya = ya + jnp.dot(m, xm, preferred_element_type=F32)
            ya = ya + jnp.dot(ce, sm, preferred_element_type=F32)
        ys.append(ya)
    y = jnp.concatenate(ys, axis=1) + hx_ref[2:3, :] * xs

    new_st = st * jnp.exp(tot) + jnp.dot(bm.T.astype(BF16), wgt.astype(BF16), preferred_element_type=F32)
    st_ref[...] = new_st

    @pl.when(c == nc - 1)
    def _():
        so_ref[...] = new_st.T.reshape(GROUP_W // SSM_HEAD_DIM, SSM_HEAD_DIM, D_STATE)

    z = z_ref[...]
    g = y[0:qr] * (z * jax.nn.sigmoid(z))
    ms = jnp.mean(g * g, axis=-1, keepdims=True)
    y_ref[...] = g * lax.rsqrt(ms + RMS_EPS) * hx_ref[3:4, :]


def ssd(yin, prev, cw, hp, hx, s0, nb, seq):
    if seq % CHUNK == 0:
        q, qr, nc = CHUNK, CHUNK, seq // CHUNK
    else:
        assert seq <= SSD_SHORT and seq % 8 == 0
        q, qr, nc = SSD_SHORT, seq, 1
    has_init = s0 is not None
    cw_w = GROUP_W + 2 * D_STATE
    hpg = SSM_HEADS // SSM_GROUPS

    def row(b, g, c):
        return b * nc + c

    in_specs = [
        pl.BlockSpec((qr, GROUP_W), lambda b, g, c: (row(b, g, c), XS0 // GROUP_W + g)),
        pl.BlockSpec((qr, D_STATE), lambda b, g, c: (row(b, g, c), B0 // D_STATE + g)),
        pl.BlockSpec((qr, D_STATE), lambda b, g, c: (row(b, g, c), C0 // D_STATE + g)),
        pl.BlockSpec((qr, GROUP_W), lambda b, g, c: (row(b, g, c), Z0 // GROUP_W + g)),
        pl.BlockSpec((qr, LANES), lambda b, g, c: (row(b, g, c), DTC0 // LANES + g)),
        pl.BlockSpec((qr, GROUP_W), lambda b, g, c: (row(b, g, c), DTX0 // GROUP_W + g)),
        pl.BlockSpec((None, None, 8, cw_w), lambda b, g, c: (b, g, 0, 0)),
        pl.BlockSpec((None, 8, cw_w), lambda b, g, c: (g, 0, 0)),
        pl.BlockSpec((None, 8, LANES), lambda b, g, c: (g, 0, 0)),
        pl.BlockSpec((None, 8, GROUP_W), lambda b, g, c: (g, 0, 0)),
    ]
    args = [yin, yin, yin, yin, yin, yin, prev, cw, hp, hx]
    if has_init:
        in_specs.append(pl.BlockSpec((None, hpg, SSM_HEAD_DIM, D_STATE), lambda b, g, c: (b, g, 0, 0)))
        args.append(s0)
    return pl.pallas_call(
        functools.partial(_ssd_kernel, q=q, qr=qr, nc=nc, has_init=has_init),
        grid=(nb, SSM_GROUPS, nc),
        in_specs=in_specs,
        out_specs=[pl.BlockSpec((qr, GROUP_W), lambda b, g, c: (row(b, g, c), g)),
                   pl.BlockSpec((None, hpg, SSM_HEAD_DIM, D_STATE), lambda b, g, c: (b, g, 0, 0))],
        out_shape=[jax.ShapeDtypeStruct((nb * seq, D_INNER), F32),
                   jax.ShapeDtypeStruct((nb, SSM_HEADS, SSM_HEAD_DIM, D_STATE), F32)],
        scratch_shapes=[pltpu.VMEM((q + 8, cw_w), F32),
                        pltpu.VMEM((8, cw_w), F32),
                        pltpu.VMEM((D_STATE, GROUP_W), F32),
                        pltpu.VMEM((q, LANES + GROUP_W), F32)],
        compiler_params=_cparams(("parallel", "parallel", "arbitrary")),
        name="ssd",
    )(*args)


SB_KBLK = 256
SB_QBLK = 512


def _sb_prompt_kernel(bias_ref, q_ref, k_ref, v_ref, o_ref):
    hp = pl.program_id(1)
    qi = pl.program_id(2)
    tq = SB_QBLK
    kb = SB_KBLK
    scale = SB_HEAD_DIM ** -0.5
    lane = lax.broadcasted_iota(jnp.int32, (1, LANES), 1)
    q = q_ref[...] * scale
    qstack = jnp.concatenate([jnp.where(lane // SB_HEAD_DIM == hh, q, 0.0) for hh in range(2)],
                             axis=0).astype(BF16)
    bias = [bias_ref[2 * hp + hh] for hh in range(2)]
    upper = (lax.broadcasted_iota(jnp.int32, (kb, kb), 0) >
             lax.broadcasted_iota(jnp.int32, (kb, kb), 1)).astype(BF16)
    qpos = qi * tq + lax.broadcasted_iota(jnp.int32, (tq, 1), 0)
    kcol = lax.broadcasted_iota(jnp.int32, (1, kb), 1)

    def block(kblk, carry, masked):
        runs, acc = carry
        start = pl.multiple_of(kblk * kb, kb)
        k = k_ref[pl.ds(start, kb), :].astype(BF16)
        v = v_ref[pl.ds(start, kb), :]
        vis = (start + kcol) < qpos
        zz = lax.dot_general(qstack, k, (((1,), (1,)), ((), ())), preferred_element_type=F32)
        lss, lks = [], []
        for hh in range(2):
            z = zz[hh * tq:(hh + 1) * tq] + bias[hh]
            ls = _log_sigmoid(z)
            lk = ls - z
            if masked:
                lk = jnp.where(vis, lk, 0.0)
            lss.append(ls)
            lks.append(lk)
        parts = [_split_bf16(lk) for lk in lks]
        stack = jnp.concatenate([p[0] for p in parts] + [p[1] for p in parts], axis=0)
        later = jnp.dot(stack, upper, preferred_element_type=F32)
        ws, vms, new_runs = [], [], []
        for hh in range(2):
            w = jnp.exp(lss[hh] + later[hh * tq:(hh + 1) * tq] + later[(2 + hh) * tq:(3 + hh) * tq] + runs[hh])
            if masked:
                w = jnp.where(vis, w, 0.0)
            ws.append(w.astype(BF16))
            vms.append(jnp.where(lane // SB_HEAD_DIM == hh, v, 0.0).astype(BF16))
            new_runs.append(runs[hh] + jnp.sum(lks[hh], axis=1, keepdims=True))
        acc = acc + jnp.dot(jnp.concatenate(ws, axis=1), jnp.concatenate(vms, axis=0),
                            preferred_element_type=F32)
        return tuple(new_runs), acc

    zero = jnp.zeros((tq, 1), F32)
    first = (qi * tq) // kb
    carry = ((zero, zero), jnp.zeros((tq, LANES), F32))
    for m in reversed(range(tq // kb)):
        carry = block(first + m, carry, True)

    def body(i, carry):
        return block(first - 1 - i, carry, False)

    carry = lax.fori_loop(0, first, body, carry)
    o_ref[...] = carry[1]


def sb_prompt(yin, bias, nb, seq):
    assert seq % SB_KBLK == 0 and seq % SB_QBLK == 0
    nq = seq // SB_QBLK
    return pl.pallas_call(
        _sb_prompt_kernel,
        grid=(nb, SB_HEADS // 2, nq),
        in_specs=[pl.BlockSpec(memory_space=pltpu.SMEM),
                  pl.BlockSpec((SB_QBLK, LANES), lambda b, h, i: (b * nq + i, Q0 // LANES + h)),
                  pl.BlockSpec((seq, LANES), lambda b, h, i: (b, K0 // LANES + h)),
                  pl.BlockSpec((seq, LANES), lambda b, h, i: (b, V0 // LANES + h))],
        out_specs=pl.BlockSpec((SB_QBLK, LANES), lambda b, h, i: (b * nq + i, h)),
        out_shape=jax.ShapeDtypeStruct((nb * seq, SB_HEADS * SB_HEAD_DIM), F32),
        compiler_params=_cparams(("parallel", "parallel", "arbitrary")),
        name="sb_prompt",
    )(bias, yin, yin, yin)


SB_PAGES_PER_STEP = 8


def _sb_sample_kernel(pt_ref, bias_ref, q_ref, kn_ref, vn_ref, *refs, npp, dq):
    krefs = refs[:npp]
    vrefs = refs[npp:2 * npp]
    o_ref = refs[2 * npp]
    qbt_ref, run_ref, acc_ref = refs[2 * npp + 1:]
    j = pl.program_id(1)
    width = SB_HEADS * SB_HEAD_DIM
    hq = SB_HEADS * dq
    page = CHUNK
    blk = 2 * page
    nblk = npp // 2
    rowh = lax.broadcasted_iota(jnp.int32, (hq, 1), 0) // dq
    colh = lax.broadcasted_iota(jnp.int32, (1, width), 1) // SB_HEAD_DIM
    diag = rowh == colh
    upper = (lax.broadcasted_iota(jnp.int32, (blk, blk), 0) >
             lax.broadcasted_iota(jnp.int32, (blk, blk), 1)).astype(BF16)

    @pl.when(j == 0)
    def _():
        qs = q_ref[...] * (SB_HEAD_DIM ** -0.5)
        qrep = jnp.concatenate([qs] * SB_HEADS, axis=0)
        qb = jnp.where(diag, qrep, 0.0).astype(BF16)
        qbt_ref[...] = qb
        pad = jnp.zeros((page - dq, width), F32)
        kn = jnp.concatenate([kn_ref[...], pad], axis=0).astype(BF16)
        vnt = jnp.concatenate([vn_ref[...], pad], axis=0).T.astype(BF16)
        z = lax.dot_general(qb, kn, (((1,), (1,)), ((), ())), preferred_element_type=F32) + bias_ref[...]
        vis = (lax.broadcasted_iota(jnp.int32, (1, page), 1) <
               lax.broadcasted_iota(jnp.int32, (hq, 1), 0) % dq)
        ls = _log_sigmoid(z)
        lk = jnp.where(vis, ls - z, 0.0)
        hi, lo = _split_bf16(lk)
        later = (jnp.dot(hi, upper[:page, :page], preferred_element_type=F32) +
                 jnp.dot(lo, upper[:page, :page], preferred_element_type=F32))
        w = jnp.where(vis, jnp.exp(ls + later), 0.0)
        acc_ref[...] = lax.dot_general(vnt, w.astype(BF16), (((1,), (1,)), ((), ())),
                                       preferred_element_type=F32)
        run_ref[...] = jnp.sum(lk, axis=1, keepdims=True)

    kcat = jnp.concatenate([krefs[i][...].astype(BF16) for i in range(npp)], axis=1)
    z = jnp.dot(qbt_ref[...], kcat, preferred_element_type=F32) + bias_ref[...]
    ls = _log_sigmoid(z)
    lk = ls - z
    hi, lo = _split_bf16(lk)
    stack = jnp.concatenate([hi[:, b * blk:(b + 1) * blk] for b in range(nblk)] +
                            [lo[:, b * blk:(b + 1) * blk] for b in range(nblk)], axis=0)
    later = jnp.dot(stack, upper, preferred_element_type=F32)
    run = run_ref[...]
    ws = [None] * nblk
    for b in reversed(range(nblk)):
        cs = slice(b * blk, (b + 1) * blk)
        later_b = later[b * hq:(b + 1) * hq] + later[(nblk + b) * hq:(nblk + b + 1) * hq]
        ws[b] = jnp.exp(ls[:, cs] + later_b + run).astype(BF16)
        run = run + jnp.sum(lk[:, cs], axis=1, keepdims=True)
    run_ref[...] = run
    wcat = jnp.concatenate(ws, axis=1)
    vcat = jnp.concatenate([vrefs[i][...].astype(BF16) for i in range(npp)], axis=1)
    acc_ref[...] += lax.dot_general(vcat, wcat, (((1,), (1,)), ((), ())), preferred_element_type=F32)

    @pl.when(j == pl.num_programs(1) - 1)
    def _():
        sel = jnp.where(diag, acc_ref[...].T, 0.0)
        o = sel[0:dq]
        for h in range(1, SB_HEADS):
            o = o + sel[h * dq:(h + 1) * dq]
        o_ref[...] = o


def sb_sample(yin, bias, cache_k, cache_v, page_table, nb, dq):
    n_pool, page = cache_k.shape[0], cache_k.shape[1]
    assert page == CHUNK and dq == 8
    width = SB_HEADS * SB_HEAD_DIM
    npages = page_table.shape[1]
    npp = SB_PAGES_PER_STEP
    assert npages % npp == 0
    hq = SB_HEADS * dq
    ck = jnp.transpose(cache_k, (0, 2, 3, 1)).reshape(n_pool, width, page)
    cv = jnp.transpose(cache_v, (0, 2, 3, 1)).reshape(n_pool, width, page)
    bias_l = jnp.repeat(bias, dq).reshape(hq, 1)

    def page_spec(i):
        return pl.BlockSpec((None, width, page),
                            lambda b, j, pt: (pt[b, npages - (j + 1) * npp + i], 0, 0))

    grid_spec = pltpu.PrefetchScalarGridSpec(
        num_scalar_prefetch=1,
        grid=(nb, npages // npp),
        in_specs=[pl.BlockSpec((hq, 1), lambda b, j, pt: (0, 0)),
                  pl.BlockSpec((dq, width), lambda b, j, pt: (b, Q0 // width)),
                  pl.BlockSpec((dq, width), lambda b, j, pt: (b, K0 // width)),
                  pl.BlockSpec((dq, width), lambda b, j, pt: (b, V0 // width))]
                 + [page_spec(i) for i in range(npp)] + [page_spec(i) for i in range(npp)],
        out_specs=pl.BlockSpec((dq, width), lambda b, j, pt: (b, 0)),
        scratch_shapes=[pltpu.VMEM((hq, width), BF16),
                        pltpu.VMEM((hq, 1), F32),
                        pltpu.VMEM((width, hq), F32)],
    )
    return pl.pallas_call(
        functools.partial(_sb_sample_kernel, npp=npp, dq=dq),
        grid_spec=grid_spec,
        out_shape=jax.ShapeDtypeStruct((nb * dq, width), F32),
        compiler_params=_cparams(("parallel", "arbitrary")),
        name="sb_sample",
    )(page_table, bias_l, yin, yin, yin, *([ck] * npp), *([cv] * npp))


MEM_DCHUNKS = MEM_HEAD_DIM // LANES


def _mem_attend_kernel(q_ref, k_ref, v_ref, w_ref, o_ref, *, tiled):
    def head(ref, h):
        if not tiled:
            return ref[:, h * MEM_HEAD_DIM:(h + 1) * MEM_HEAD_DIM]
        rows = ref.shape[0] // (MEM_DCHUNKS * MEM_HEADS)
        return jnp.concatenate(
            [ref[pl.ds(dc * MEM_HEADS + h, rows, stride=MEM_DCHUNKS * MEM_HEADS), :] for dc in range(MEM_DCHUNKS)],
            axis=1)

    for h in range(MEM_HEADS):
        sl = slice(h * MEM_HEAD_DIM, (h + 1) * MEM_HEAD_DIM)
        q = q_ref[:, sl]
        ms = jnp.mean(q * q, axis=-1, keepdims=True)
        qn = (q * lax.rsqrt(ms + RMS_EPS) * w_ref[...]).astype(BF16)
        s = lax.dot_general(qn, head(k_ref, h).astype(BF16), (((1,), (1,)), ((), ())),
                            preferred_element_type=F32) * (MEM_HEAD_DIM ** -0.5)
        e = jnp.exp(s - jnp.max(s, axis=-1, keepdims=True))
        p = e / jnp.sum(e, axis=-1, keepdims=True)
        o_ref[:, sl] = jnp.dot(p.astype(BF16), head(v_ref, h).astype(BF16), preferred_element_type=F32)


def _mem_tiled_view(cache):
    nb, mt = cache.shape[0], cache.shape[1]
    v = cache.reshape(nb, mt, MEM_HEADS, MEM_DCHUNKS, LANES)
    return jnp.transpose(v, (0, 1, 3, 2, 4)).reshape(nb, mt * MEM_DCHUNKS * MEM_HEADS, LANES)


def mem_attend(yin, mem_k, mem_v, qnw, nb, seq, tq, tiled=False):
    width = MEM_HEADS * MEM_HEAD_DIM
    nq = seq // tq
    kv_block = (None,) + tuple(mem_k.shape[1:])
    return pl.pallas_call(
        functools.partial(_mem_attend_kernel, tiled=tiled),
        grid=(nb, nq),
        in_specs=[pl.BlockSpec((tq, width), lambda b, i: (b * nq + i, QM0 // width)),
                  pl.BlockSpec(kv_block, lambda b, i: (b, 0, 0)),
                  pl.BlockSpec(kv_block, lambda b, i: (b, 0, 0)),
                  pl.BlockSpec((1, MEM_HEAD_DIM), lambda b, i: (0, 0))],
        out_specs=pl.BlockSpec((tq, width), lambda b, i: (b * nq + i, 0)),
        out_shape=jax.ShapeDtypeStruct((nb * seq, width), F32),
        compiler_params=_cparams(("parallel", "arbitrary")),
        name="mem_attend",
    )(yin, mem_k, mem_v, qnw.reshape(1, MEM_HEAD_DIM))


def _merge_kernel(x_ref, y_ref, sb_ref, mm_ref, g1_ref, g2_ref, g3_ref, w1_ref, w2_ref, w3_ref, wo_ref, o_ref):
    b1 = jnp.dot(y_ref[...].astype(BF16), w1_ref[...], preferred_element_type=F32)
    b2 = jnp.dot(sb_ref[...].astype(BF16), w2_ref[...], preferred_element_type=F32)
    b3 = jnp.dot(mm_ref[...].astype(BF16), w3_ref[...], preferred_element_type=F32)
    merged = (jax.nn.sigmoid(g1_ref[...]) * b1 + jax.nn.sigmoid(g2_ref[...]) * b2 +
              jax.nn.sigmoid(g3_ref[...]) * b3)
    o_ref[...] = x_ref[...] + jnp.dot(merged.astype(BF16), wo_ref[...], preferred_element_type=F32)


def merge(x, yin, y_ssm, o_sb, o_mem, w1, w2, w3, wo, tm):
    t, d = x.shape

    def tok(width, col=0):
        return pl.BlockSpec((tm, width), lambda i: (i, col))

    def full(w):
        return pl.BlockSpec(w.shape, lambda i: (0, 0))

    return pl.pallas_call(
        _merge_kernel,
        grid=(t // tm,),
        in_specs=[tok(d), tok(D_INNER), tok(d), tok(d),
                  tok(d, G0 // d), tok(d, G0 // d + 1), tok(d, G0 // d + 2),
                  full(w1), full(w2), full(w3), full(wo)],
        out_specs=tok(d),
        out_shape=jax.ShapeDtypeStruct((t, d), F32),
        compiler_params=_cparams(("parallel",)),
        name="merge",
    )(x, y_ssm, o_sb, o_mem, yin, yin, yin, w1, w2, w3, wo)


PEER_TB = 256


def _top_values(s, n):
    out = []
    for _ in range(n):
        m = jnp.max(s, axis=0, keepdims=True)
        out.append(m)
        s = jnp.where(s == m, NEG, s)
    return out


def _peer_stats_kernel(q_ref, keys_ref, th_ref, p0_ref, p1_ref):
    tb = q_ref.shape[0]
    big = -NEG
    for h in range(PEER_HEADS):
        tops, scores = [], []
        for c in range(2):
            col = (h * 2 + c) * N_KEYS
            s = lax.dot_general(keys_ref[c], q_ref[:, col:col + N_KEYS].astype(BF16),
                                (((1,), (1,)), ((), ())), preferred_element_type=F32)
            scores.append(s)
            tops.append(_top_values(s, PEER_TOPK))
        b = jnp.concatenate(tops[1], axis=0)
        row8 = lax.broadcasted_iota(jnp.int32, (8, 1), 0)
        pieces = [tops[0][0] + b]
        for a in range(1, 8):
            part = tops[0][a] + b[0:8]
            if PEER_TOPK // (a + 1) < 8:
                part = jnp.where(row8 < PEER_TOPK // (a + 1), part, NEG)
            pieces.append(part)
        pieces.append(jnp.concatenate(tops[0][8:], axis=0) + tops[1][0])
        cand = jnp.concatenate(pieces, axis=0)
        best = _top_values(cand, PEER_TOPK)
        zsum = jnp.zeros((1, tb), F32)
        for m in best:
            zsum = zsum + jnp.exp(m - best[0])
        tau = best[-1]
        iz = 1.0 / zsum
        pb = jnp.exp(b - tops[1][0]) * iz
        thetas = [jnp.min(jnp.where(pieces[0] >= tau, pb, big), axis=0, keepdims=True)]
        for a in range(1, 8):
            thetas.append(jnp.min(jnp.where(pieces[a] >= tau, pb[0:8], big), axis=0, keepdims=True))
        tail = jnp.where(pieces[8] >= tau, pb[0:1], big)
        thetas += [tail[r:r + 1] for r in range(8)]
        theta = jnp.full((N_KEYS, tb), big, F32)
        for a in range(PEER_TOPK):
            theta = jnp.where(scores[0] == tops[0][a], thetas[a], theta)
        th_ref[h] = theta
        p0_ref[h] = jnp.exp(scores[0] - tops[0][0])
        p1_ref[h] = jnp.exp(scores[1] - tops[1][0]) * iz


def peer_stats(qp, keys):
    t = qp.shape[0]
    tb = PEER_TB
    return pl.pallas_call(
        _peer_stats_kernel,
        grid=(t // tb,),
        in_specs=[pl.BlockSpec((tb, qp.shape[1]), lambda i: (i, 0)),
                  pl.BlockSpec(keys.shape, lambda i: (0, 0, 0))],
        out_specs=[pl.BlockSpec((PEER_HEADS, N_KEYS, tb), lambda i: (0, 0, i))] * 3,
        out_shape=[jax.ShapeDtypeStruct((PEER_HEADS, N_KEYS, t), F32)] * 3,
        compiler_params=_cparams(("parallel",)),
        name="peer_stats",
    )(qp, keys)


PEER_TBK = 512
PEER_EB = 512
PEER_SUB = 32


def _peer_expert_kernel(x_ref, nw_ref, wu_ref, wd_ref, th_ref, p0_ref, p1_ref, o_ref,
                        xt_ref, acc_ref, act_ref, coef_ref, rows_ref):
    e = pl.program_id(1)
    ne = pl.num_programs(1)
    nsub = PEER_EB // N_KEYS

    @pl.when(e == 0)
    def _():
        x = x_ref[...]
        ms = jnp.mean(x * x, axis=-1, keepdims=True)
        xt_ref[...] = (x * lax.rsqrt(ms + RMS_EPS) * nw_ref[...]).T.astype(BF16)
        acc_ref[...] = jnp.zeros_like(acc_ref)

    act_ref[...] = jnp.dot(wu_ref[...], xt_ref[...], preferred_element_type=F32)
    for ii in range(nsub):
        i = e * nsub + ii
        for h in range(PEER_HEADS):
            rows_ref[h:h + 1, :] = th_ref[h, pl.ds(i, 1), :]
            rows_ref[PEER_HEADS + h:PEER_HEADS + h + 1, :] = p0_ref[h, pl.ds(i, 1), :]
        for tc in range(x_ref.shape[0] // LANES):
            cs = slice(tc * LANES, (tc + 1) * LANES)
            for sub in range(N_KEYS // PEER_SUB):
                js = slice(sub * PEER_SUB, (sub + 1) * PEER_SUB)
                es = slice(ii * N_KEYS + sub * PEER_SUB, ii * N_KEYS + (sub + 1) * PEER_SUB)
                gate = jnp.zeros((PEER_SUB, LANES), F32)
                for h in range(PEER_HEADS):
                    p1 = p1_ref[h, js, cs]
                    pr = rows_ref[PEER_HEADS + h:PEER_HEADS + h + 1, cs] * p1
                    gate = gate + jnp.where(p1 >= rows_ref[h:h + 1, cs], pr, 0.0)
                a = act_ref[es, cs]
                gelu = 0.5 * a * (1.0 + lax.erf(a * (2.0 ** -0.5)))
                coef_ref[es, cs] = (gate * gelu).astype(BF16)
    acc_ref[...] += jnp.dot(wd_ref[...], coef_ref[...], preferred_element_type=F32)

    @pl.when(e == ne - 1)
    def _():
        o_ref[...] = x_ref[...] + acc_ref[...].T


def peer_experts(x, nw, wu, wdt, stats):
    t, d = x.shape
    tb, eb = PEER_TBK, PEER_EB
    ne = wu.shape[0] // eb
    return pl.pallas_call(
        _peer_expert_kernel,
        grid=(t // tb, ne),
        in_specs=[pl.BlockSpec((tb, d), lambda i, e: (i, 0)),
                  pl.BlockSpec((1, d), lambda i, e: (0, 0)),
                  pl.BlockSpec((eb, d), lambda i, e: (e, 0)),
                  pl.BlockSpec((d, eb), lambda i, e: (0, e))]
                 + [pl.BlockSpec((PEER_HEADS, N_KEYS, tb), lambda i, e: (0, 0, i))] * 3,
        out_specs=pl.BlockSpec((tb, d), lambda i, e: (i, 0)),
        out_shape=jax.ShapeDtypeStruct((t, d), F32),
        scratch_shapes=[pltpu.VMEM((d, tb), BF16),
                        pltpu.VMEM((d, tb), F32),
                        pltpu.VMEM((eb, tb), F32),
                        pltpu.VMEM((eb, tb), BF16),
                        pltpu.VMEM((2 * PEER_HEADS, tb), F32)],
        compiler_params=_cparams(("parallel", "arbitrary")),
        name="peer_experts",
    )(x, nw.reshape(1, d), wu, wdt, *stats)


def _pad_rows(x, mult):
    t = x.shape[0]
    pad = (-t) % mult
    return x if pad == 0 else jnp.pad(x, ((0, pad), (0, 0)))


def _pick(t, pref):
    return pref if t % pref == 0 else t


def _peer(x1, wts):
    t = x1.shape[0]
    xp = _pad_rows(x1, PEER_TBK)
    qp = norm_matmul(xp, wts["norm2_w"], wts["peer_w_q"], _pick(xp.shape[0], 1024), 512)
    stats = peer_stats(qp, wts["peer_keys"])
    out = peer_experts(xp, wts["norm2_w"], wts["peer_w_up"], wts["peer_w_down_t"], stats)
    return out[:t]


def _layer(x, mem_k, mem_v, conv_prev, ssm_prev, wts, sb_fn, mem_tiled):
    nb, seq, d = x.shape
    t = nb * seq
    xf = x.reshape(t, d)
    yin = norm_matmul(xf, wts["norm1_w"], wts["w_in"], _pick(t, 2048), 512)
    y_ssm, ssm_new = ssd(yin, conv_prev, wts["conv_wg"], wts["ssd_hp"], wts["ssd_hx"], ssm_prev, nb, seq)
    o_sb = sb_fn(yin)
    o_mem = mem_attend(yin, mem_k, mem_v, wts["q_norm_w"], nb, seq, _pick(seq, 512), mem_tiled)
    x1 = merge(xf, yin, y_ssm, o_sb, o_mem, wts["w_ssm_out"], wts["w_sb_out"], wts["w_mem_out"], wts["w_out"],
               _pick(t, 256))
    y = _peer(x1, wts)
    y3 = yin.reshape(nb, seq, N_IN)
    k_new = y3[:, :, K0:K0 + d].reshape(nb, seq, SB_HEADS, SB_HEAD_DIM)
    v_new = y3[:, :, V0:V0 + d].reshape(nb, seq, SB_HEADS, SB_HEAD_DIM)
    xbc_raw = y3[:, :, XS0:XS0 + CONV_DIM]
    return y.reshape(nb, seq, d), k_new, v_new, xbc_raw, ssm_new


def _group_cols(a):
    lead = a.shape[:-1]
    xs = a[..., :D_INNER].reshape(lead + (SSM_GROUPS, GROUP_W))
    bm = a[..., D_INNER:D_INNER + SSM_GROUPS * D_STATE].reshape(lead + (SSM_GROUPS, D_STATE))
    cm = a[..., D_INNER + SSM_GROUPS * D_STATE:].reshape(lead + (SSM_GROUPS, D_STATE))
    return jnp.concatenate([xs, bm, cm], axis=-1)


def _prev_rows(prev):
    g = jnp.moveaxis(_group_cols(prev), 2, 1)
    return jnp.pad(g, ((0, 0), (0, 0), (8 - (CONV_W - 1), 0), (0, 0)))


def _prep_weights(l, norm1_w, w_in, conv_w, conv_b, dt_bias, a_log, d_skip, ssm_norm_w, w_ssm_out, w_sb_out,
                  sb_bias, mem_norm_w, w_mem_kv, q_norm_w, k_norm_w, w_mem_out, w_out, norm2_w, peer_w_q,
                  peer_sub_keys, peer_w_up, peer_w_down):
    hpg = SSM_HEADS // SSM_GROUPS
    w = w_in[l]
    dt_cols = w[:, D_INNER + CONV_DIM:D_INNER + CONV_DIM + SSM_HEADS]
    rest = w[:, D_INNER + CONV_DIM + SSM_HEADS:]
    dtc = jnp.pad(dt_cols.reshape(D_MODEL, SSM_GROUPS, hpg), ((0, 0), (0, 0), (0, LANES - hpg)))
    w_fused = jnp.concatenate([w[:, :D_INNER + CONV_DIM], rest, dtc.reshape(D_MODEL, SSM_GROUPS * LANES),
                               jnp.repeat(dt_cols, SSM_HEAD_DIM, axis=1)], axis=1).astype(BF16)
    assert w_fused.shape[1] == N_IN

    def compact(v):
        return jnp.pad(v.reshape(SSM_GROUPS, hpg), ((0, 0), (0, LANES - hpg)))

    def expand(v):
        return jnp.repeat(v, SSM_HEAD_DIM).reshape(SSM_GROUPS, GROUP_W)

    zc = jnp.zeros((SSM_GROUPS, LANES), F32)
    zx = jnp.zeros((SSM_GROUPS, GROUP_W), F32)
    ssd_hp = jnp.stack([compact(dt_bias[l]), compact(a_log[l])] + [zc] * 6, axis=1)
    ssd_hx = jnp.stack([expand(dt_bias[l]), expand(a_log[l]), expand(d_skip[l]),
                        ssm_norm_w[l].reshape(SSM_GROUPS, GROUP_W)] + [zx] * 4, axis=1)
    conv_rows = jnp.concatenate([conv_w[l], conv_b[l][None]], axis=0)
    conv_wg = jnp.pad(jnp.moveaxis(_group_cols(conv_rows), 1, 0), ((0, 0), (0, 8 - (CONV_W + 1)), (0, 0)))
    return dict(
        norm1_w=norm1_w[l], w_in=w_fused, conv_wg=conv_wg, ssd_hp=ssd_hp, ssd_hx=ssd_hx,
        w_ssm_out=w_ssm_out[l].astype(BF16), w_sb_out=w_sb_out[l].astype(BF16), sb_bias=sb_bias[l],
        mem_norm_w=mem_norm_w[l], w_mem_kv=w_mem_kv[l].astype(BF16), q_norm_w=q_norm_w[l],
        k_norm_w=k_norm_w[l], w_mem_out=w_mem_out[l].astype(BF16), w_out=w_out[l].astype(BF16),
        norm2_w=norm2_w[l], peer_w_q=peer_w_q[l].astype(BF16), peer_keys=peer_sub_keys[l].astype(BF16),
        peer_w_up=peer_w_up[l].astype(BF16), peer_w_down_t=peer_w_down[l].T.astype(BF16))


def kernel(x_prompt, x_sample, cache_sb_k, cache_sb_v, cache_mem_k, cache_mem_v, state_conv, state_ssm, page_table, mem_prompt, norm1_w, w_in, conv_w, conv_b, dt_bias, a_log, d_skip, ssm_norm_w, w_ssm_out, w_sb_out, sb_bias, mem_norm_w, w_mem_kv, q_norm_w, k_norm_w, w_mem_out, w_out, norm2_w, peer_w_q, peer_sub_keys, peer_w_up, peer_w_down):
    bp, seq_p, d = x_prompt.shape
    bs, seq_s, _ = x_sample.shape
    depth = w_in.shape[0]
    mt = mem_prompt.shape[1]
    mem_w = MEM_HEADS * MEM_HEAD_DIM
    y_p, y_s = x_prompt, x_sample
    outs = [[] for _ in range(10)]
    for l in range(depth):
        wts = _prep_weights(l, norm1_w, w_in, conv_w, conv_b, dt_bias, a_log, d_skip, ssm_norm_w, w_ssm_out,
                            w_sb_out, sb_bias, mem_norm_w, w_mem_kv, q_norm_w, k_norm_w, w_mem_out, w_out,
                            norm2_w, peer_w_q, peer_sub_keys, peer_w_up, peer_w_down)
        kv = norm_matmul(mem_prompt.reshape(bp * mt, d), wts["mem_norm_w"], wts["w_mem_kv"],
                         _pick(bp * mt, 1024), 512)
        mk = headnorm(kv, wts["k_norm_w"], _pick(bp * mt, 512)).reshape(bp, mt, mem_w)
        mv = kv[:, mem_w:].reshape(bp, mt, mem_w)
        prev0 = jnp.zeros((bp, SSM_GROUPS, 8, GROUP_W + 2 * D_STATE), F32)
        y_p, kp, vp, xbc_p, sp = _layer(
            y_p, mk, mv, prev0, None, wts,
            lambda yin: sb_prompt(yin, wts["sb_bias"], bp, seq_p), False)
        y_s, ks, vs, xbc_s, ss = _layer(
            y_s, _mem_tiled_view(cache_mem_k[l]), _mem_tiled_view(cache_mem_v[l]),
            _prev_rows(state_conv[l]), state_ssm[l], wts,
            lambda yin: sb_sample(yin, wts["sb_bias"], cache_sb_k[l], cache_sb_v[l], page_table, bs, seq_s), True)
        conv_s = jnp.concatenate([state_conv[l], xbc_s], axis=1)[:, -(CONV_W - 1):]
        for lst, v in zip(outs, [kp, vp, mk.reshape(bp, mt, MEM_HEADS, MEM_HEAD_DIM),
                                 mv.reshape(bp, mt, MEM_HEADS, MEM_HEAD_DIM), xbc_p[:, -(CONV_W - 1):], sp,
                                 ks, vs, conv_s, ss]):
            lst.append(v)
    return (y_p, y_s) + tuple(jnp.stack(o) for o in outs)
```
